```python
import jax, jax.numpy as jnp
from jax import lax
import numpy as np

D_MODEL = 4096
BATCH = 8
SEQ = 2048
DEPTH = 4

HEAD_DIM = 128
MIX_WIDTH = D_MODEL
N_GROUPS = 4
GROUP_WIDTH = MIX_WIDTH // N_GROUPS
NA_HEADS = GROUP_WIDTH // HEAD_DIM
GRID_W = 64
NA_ROWS_MAX = 8
NA_COLS = 16
CONV_CHANNELS = GROUP_WIDTH
CONV_WIDTH = 31
SWA_HEADS = GROUP_WIDTH // HEAD_DIM
SWA_KV_HEADS = 2
SWA_RADIUS = 128
DIL_HEADS = GROUP_WIDTH // HEAD_DIM
DIL_PATTERNS = ((128, 1), (512, 4), (2048, 16))
FFN_DIM = D_MODEL * 3 // 2
KV_WIDTH = SWA_KV_HEADS * HEAD_DIM
SPLIT_SIZES = (GROUP_WIDTH, GROUP_WIDTH, GROUP_WIDTH,
               2 * CONV_CHANNELS,
               GROUP_WIDTH, KV_WIDTH, KV_WIDTH,
               GROUP_WIDTH, GROUP_WIDTH, GROUP_WIDTH)
IN_COLS = sum(SPLIT_SIZES)
EPS = 1e-6
NEG = -1e30

kernel_name = "hybrid_parallel_group_encoder"


def rms_norm(x, g):
    xf = x.astype(jnp.float32)
    y = xf * lax.rsqrt(jnp.mean(xf * xf, axis=-1, keepdims=True) + EPS)
    return (y * g.astype(jnp.float32)).astype(x.dtype)


def swiglu(h, w_in, w_out):
    gate, up = jnp.split(h @ w_in, 2, axis=-1)
    return (jax.nn.silu(gate) * up) @ w_out


def split_heads(t, n_heads):
    n, s, _ = t.shape
    return t.reshape(n, s, n_heads, -1).transpose(0, 2, 1, 3)


def merge_heads(t):
    n, h, s, d = t.shape
    return t.transpose(0, 2, 1, 3).reshape(n, s, h * d)


def alibi_slopes(n):
    return 2.0 ** (-8.0 * jnp.arange(1, n + 1, dtype=jnp.float32) / n)


def banded_attention(q, k, v, slopes, radius, spacing):
    n, hq, L, hd = q.shape
    hkv = k.shape[1]
    g = hq // hkv
    blk = radius
    nb = -(-L // blk)
    lp = nb * blk
    qb = jnp.pad(q, ((0, 0), (0, 0), (0, lp - L), (0, 0))).reshape(n, hkv, g, nb, blk, hd)

    def key_blocks(t):
        tb = jnp.pad(t, ((0, 0), (0, 0), (blk, lp - L + blk), (0, 0))).reshape(n, hkv, nb + 2, blk, hd)
        return jnp.concatenate([tb[:, :, :-2], tb[:, :, 1:-1], tb[:, :, 2:]], axis=3)

    kb, vb = key_blocks(k), key_blocks(v)
    s = jnp.einsum('nkgiqd,nkisd->nkgiqs', qb, kb).astype(jnp.float32) * (hd ** -0.5)
    qpos = jnp.arange(nb)[:, None] * blk + jnp.arange(blk)[None, :]
    kpos = jnp.arange(nb)[:, None] * blk - blk + jnp.arange(3 * blk)[None, :]
    rel = kpos[:, None, :] - qpos[:, :, None]
    valid = (jnp.abs(rel) <= radius) & (kpos[:, None, :] >= 0) & (kpos[:, None, :] < L)
    dist = (spacing * jnp.abs(rel)).astype(jnp.float32)
    bias = -slopes.astype(jnp.float32).reshape(hkv, g, 1, 1, 1) * dist
    s = jnp.where(valid, s + bias, NEG)
    m = jnp.max(s, axis=-1, keepdims=True)
    p = jnp.exp(s - m)
    den = jnp.sum(p, axis=-1, keepdims=True)
    o = jnp.einsum('nkgiqs,nkisd->nkgiqd', p.astype(v.dtype), vb).astype(jnp.float32) / den
    lse = (m + jnp.log(den))[..., 0]
    o = o.reshape(n, hq, lp, hd)[:, :, :L]
    lse = lse.reshape(n, hq, lp)[:, :, :L]
    return o, lse


def neighbourhood_attention(q, k, v, rpb):
    n, h, s_len, hd = q.shape
    rows = s_len // GRID_W
    kh = min(NA_ROWS_MAX, rows)
    kw = NA_COLS
    grid = lambda t: t.reshape(n, h, rows, GRID_W, hd)
    r = jnp.arange(rows)
    row_idx = jnp.clip(r - kh // 2, 0, rows - kh)[:, None] + jnp.arange(kh)[None, :]
    kr = grid(k)[:, :, row_idx]
    vr = grid(v)[:, :, row_idx]
    c = jnp.arange(GRID_W)
    cstart = jnp.clip(c - kw // 2, 0, GRID_W - kw)
    col_ok = (c[None, :] >= cstart[:, None]) & (c[None, :] < cstart[:, None] + kw)
    dr = row_idx - r[:, None]
    dc = c[None, :] - c[:, None]
    bias = rpb[:, (dr + NA_ROWS_MAX - 1)[:, None, :, None],
               jnp.clip(dc + NA_COLS - 1, 0, 2 * NA_COLS - 2)[None, :, None, :]]
    s = jnp.einsum('nhrqd,nhrkwd->nhrqkw', grid(q), kr).astype(jnp.float32) * (hd ** -0.5)
    s = jnp.where(col_ok[:, None, :], s + bias.astype(jnp.float32), NEG)
    p = jax.nn.softmax(s.reshape(n, h, rows, GRID_W, kh * GRID_W), axis=-1)
    p = p.reshape(n, h, rows, GRID_W, kh, GRID_W)
    o = jnp.einsum('nhrqkw,nhrkwd->nhrqd', p.astype(v.dtype), vr)
    return o.reshape(n, h, s_len, hd)


def conformer_conv(u, w_dw, b_dw, ln_g, ln_b):
    a, gate = jnp.split(u, 2, axis=-1)
    h = a * jax.nn.sigmoid(gate)
    h = lax.conv_general_dilated(h, w_dw[:, None, :], window_strides=(1,),
                                 padding=[(CONV_WIDTH // 2, CONV_WIDTH // 2)],
                                 dimension_numbers=('NWC', 'WIO', 'NWC'),
                                 feature_group_count=CONV_CHANNELS) + b_dw
    hf = h.astype(jnp.float32)
    mu = jnp.mean(hf, axis=-1, keepdims=True)
    var = jnp.mean(jnp.square(hf - mu), axis=-1, keepdims=True)
    hf = (hf - mu) * lax.rsqrt(var + EPS) * ln_g.astype(jnp.float32) + ln_b.astype(jnp.float32)
    return jax.nn.silu(hf).astype(u.dtype)


def dilated_attention(q, k, v, slopes):
    n, h, s_len, hd = q.shape
    outs, lses = [], []
    for window, dil in DIL_PATTERNS:
        radius = (window // 2) // dil
        sub = s_len // dil
        to_sub = lambda t: t.reshape(n, h, sub, dil, hd).transpose(0, 3, 1, 2, 4).reshape(n * dil, h, sub, hd)
        o, lse = banded_attention(to_sub(q), to_sub(k), to_sub(v), slopes, radius, dil)
        outs.append(o.reshape(n, dil, h, sub, hd).transpose(0, 2, 3, 1, 4).reshape(n, h, s_len, hd))
        lses.append(lse.reshape(n, dil, h, sub).transpose(0, 2, 3, 1).reshape(n, h, s_len))
    w = jax.nn.softmax(jnp.stack(lses), axis=0)
    return jnp.sum(w[..., None] * jnp.stack(outs), axis=0)


def token_mixing(hn, w_in, w_out, rpb, conv_w, conv_b, conv_ln_g, conv_ln_b, sink, branch_g, sl_c, sl_d):
    u = hn @ w_in
    idx = [int(i) for i in np.cumsum(SPLIT_SIZES)[:-1]]
    aq, ak, av, bu, cq, ck, cv, dq, dk, dv = jnp.split(u, idx, axis=-1)
    oa = merge_heads(neighbourhood_attention(split_heads(aq, NA_HEADS), split_heads(ak, NA_HEADS),
                                             split_heads(av, NA_HEADS), rpb))
    ob = conformer_conv(bu, conv_w, conv_b, conv_ln_g, conv_ln_b)
    oc, lse_c = banded_attention(split_heads(cq, SWA_HEADS), split_heads(ck, SWA_KV_HEADS),
                                 split_heads(cv, SWA_KV_HEADS), sl_c, SWA_RADIUS, 1)
    oc = oc * jax.nn.sigmoid(lse_c - sink.astype(jnp.float32)[None, :, None])[..., None]
    oc = merge_heads(oc)
    od = merge_heads(dilated_attention(split_heads(dq, DIL_HEADS), split_heads(dk, DIL_HEADS),
                                       split_heads(dv, DIL_HEADS), sl_d))
    groups = [oa, ob, oc, od]
    merged = jnp.concatenate([rms_norm(o.astype(hn.dtype), branch_g[i]) for i, o in enumerate(groups)], axis=-1)
    return merged @ w_out


def setup_inputs(seed: int = 0) -> dict:
    key = jax.random.key(seed)
    ks = jax.random.split(key, 20)
    nrm = lambda k, shape, scale: jax.random.normal(k, shape, jnp.float32) * scale
    gain = lambda k, shape: 1.0 + 0.02 * jax.random.normal(k, shape, jnp.float32)
    return {
        "x": nrm(ks[0], (BATCH, SEQ, D_MODEL), 1.0),
        "ffn1_norm": gain(ks[1], (DEPTH, D_MODEL)),
        "ffn1_w_in": nrm(ks[2], (DEPTH, D_MODEL, 2 * FFN_DIM), D_MODEL ** -0.5),
        "ffn1_w_out": nrm(ks[3], (DEPTH, FFN_DIM, D_MODEL), FFN_DIM ** -0.5),
        "mix_norm": gain(ks[4], (DEPTH, D_MODEL)),
        "w_in": nrm(ks[5], (DEPTH, D_MODEL, IN_COLS), D_MODEL ** -0.5),
        "na_rpb": nrm(ks[6], (DEPTH, NA_HEADS, 2 * NA_ROWS_MAX - 1, 2 * NA_COLS - 1), 0.1),
        "conv_w": nrm(ks[7], (DEPTH, CONV_WIDTH, CONV_CHANNELS), CONV_WIDTH ** -0.5),
        "conv_b": nrm(ks[8], (DEPTH, CONV_CHANNELS), 0.02),
        "conv_ln_g": gain(ks[9], (DEPTH, CONV_CHANNELS)),
        "conv_ln_b": nrm(ks[10], (DEPTH, CONV_CHANNELS), 0.02),
        "swa_sink": nrm(ks[11], (DEPTH, SWA_HEADS), 0.5),
        "branch_norm": gain(ks[12], (DEPTH, N_GROUPS, GROUP_WIDTH)),
        "w_out": nrm(ks[13], (DEPTH, MIX_WIDTH, D_MODEL), MIX_WIDTH ** -0.5),
        "ffn2_norm": gain(ks[14], (DEPTH, D_MODEL)),
        "ffn2_w_in": nrm(ks[15], (DEPTH, D_MODEL, 2 * FFN_DIM), D_MODEL ** -0.5),
        "ffn2_w_out": nrm(ks[16], (DEPTH, FFN_DIM, D_MODEL), FFN_DIM ** -0.5),
        "final_norm": gain(ks[17], (D_MODEL,)),
    }


def reference(x, ffn1_norm, ffn1_w_in, ffn1_w_out, mix_norm, w_in, na_rpb, conv_w, conv_b,
              conv_ln_g, conv_ln_b, swa_sink, branch_norm, w_out, ffn2_norm, ffn2_w_in,
              ffn2_w_out, final_norm):
    slopes = alibi_slopes(SWA_HEADS + DIL_HEADS)
    sl_c = slopes[:SWA_HEADS]
    sl_d = slopes[SWA_HEADS:]
    h = x
    for l in range(DEPTH):
        h = h + 0.5 * swiglu(rms_norm(h, ffn1_norm[l]), ffn1_w_in[l], ffn1_w_out[l])
        h = h + token_mixing(rms_norm(h, mix_norm[l]), w_in[l], w_out[l], na_rpb[l], conv_w[l],
                             conv_b[l], conv_ln_g[l], conv_ln_b[l], swa_sink[l], branch_norm[l],
                             sl_c, sl_d)
        h = h + 0.5 * swiglu(rms_norm(h, ffn2_norm[l]), ffn2_w_in[l], ffn2_w_out[l])
    return rms_norm(h, final_norm)
```

```python
import functools

import jax
import jax.numpy as jnp
from jax import lax
from jax.experimental import pallas as pl
from jax.experimental.pallas import tpu as pltpu

D_MODEL = 4096
HEAD_DIM = 128
GROUP_WIDTH = 1024
N_HEADS = GROUP_WIDTH // HEAD_DIM
GRID_W = 64
NA_ROWS = 8
NA_COLS = 16
CONV_WIDTH = 31
CONV_HALO = 16
SWA_KV_HEADS = 2
SWA_RADIUS = 128
DIL_PATTERNS = ((128, 1), (512, 4), (2048, 16))
FFN_DIM = D_MODEL * 3 // 2
KV_WIDTH = SWA_KV_HEADS * HEAD_DIM
EPS = 1e-6
NEG = -1e30
SCALE = HEAD_DIM ** -0.5

F32 = jnp.float32
BF16 = jnp.bfloat16

COL_AQ, COL_AK, COL_AV = 0, 8, 16
COL_CQ, COL_CK, COL_CV = 24, 32, 34
COL_DQ, COL_DK, COL_DV = 36, 44, 52
ATTN_COLS = 60 * HEAD_DIM

VMEM_LIMIT = 56 * 1024 * 1024


def _params(*sem):
    return pltpu.CompilerParams(dimension_semantics=sem, vmem_limit_bytes=VMEM_LIMIT)


def _dot(a, b):
    return jnp.dot(a, b, preferred_element_type=F32)


def _dot_nt(a, b):
    return lax.dot_general(a, b, (((1,), (1,)), ((), ())), preferred_element_type=F32)


def _rms(x, g):
    ms = jnp.mean(x * x, axis=-1, keepdims=True)
    return x * lax.rsqrt(ms + EPS) * g


def _norm_proj_kernel(h_ref, g_ref, w_ref, o_ref, xn_ref):
    @pl.when(pl.program_id(1) == 0)
    def _():
        xn_ref[...] = _rms(h_ref[...], g_ref[...]).astype(BF16)

    o_ref[...] = _dot(xn_ref[...], w_ref[...]).astype(o_ref.dtype)


def norm_proj(h, g, w, out_dtype, tm=512, tn=512):
    t, d = h.shape
    n = w.shape[1]
    return pl.pallas_call(
        _norm_proj_kernel,
        grid=(t // tm, n // tn),
        in_specs=[pl.BlockSpec((tm, d), lambda i, j: (i, 0)),
                  pl.BlockSpec((1, d), lambda i, j: (0, 0)),
                  pl.BlockSpec((d, tn), lambda i, j: (0, j))],
        out_specs=pl.BlockSpec((tm, tn), lambda i, j: (i, j)),
        out_shape=jax.ShapeDtypeStruct((t, n), out_dtype),
        scratch_shapes=[pltpu.VMEM((tm, d), BF16)],
        compiler_params=_params("parallel", "arbitrary"),
        name="norm_proj",
    )(h, g.reshape(1, d), w)


def _ffn_in_kernel(h_ref, g_ref, wg_ref, wu_ref, o_ref, xn_ref):
    @pl.when(pl.program_id(1) == 0)
    def _():
        xn_ref[...] = _rms(h_ref[...], g_ref[...]).astype(BF16)

    xn = xn_ref[...]
    gate = _dot(xn, wg_ref[...])
    up = _dot(xn, wu_ref[...])
    o_ref[...] = (gate * jax.nn.sigmoid(gate) * up).astype(o_ref.dtype)


def ffn_in(h, g, w, tm=512, tf=512):
    t, d = h.shape
    f = w.shape[1] // 2
    nf = f // tf
    return pl.pallas_call(
        _ffn_in_kernel,
        grid=(t // tm, nf),
        in_specs=[pl.BlockSpec((tm, d), lambda i, j: (i, 0)),
                  pl.BlockSpec((1, d), lambda i, j: (0, 0)),
                  pl.BlockSpec((d, tf), lambda i, j: (0, j)),
                  pl.BlockSpec((d, tf), lambda i, j: (0, j + nf))],
        out_specs=pl.BlockSpec((tm, tf), lambda i, j: (i, j)),
        out_shape=jax.ShapeDtypeStruct((t, f), BF16),
        scratch_shapes=[pltpu.VMEM((tm, d), BF16)],
        compiler_params=_params("parallel", "arbitrary"),
        name="ffn_in",
    )(h, g.reshape(1, d), w, w)


def _res_out_kernel(a_ref, w_ref, h_ref, o_ref, *, scale):
    o_ref[...] = h_ref[...] + scale * _dot(a_ref[...], w_ref[...])


def res_out(a, w, h, scale, tm=1024, tn=512):
    t, k = a.shape
    n = w.shape[1]
    return pl.pallas_call(
        functools.partial(_res_out_kernel, scale=scale),
        grid=(t // tm, n // tn),
        in_specs=[pl.BlockSpec((tm, k), lambda i, j: (i, 0)),
                  pl.BlockSpec((k, tn), lambda i, j: (0, j)),
                  pl.BlockSpec((tm, tn), lambda i, j: (i, j))],
        out_specs=pl.BlockSpec((tm, tn), lambda i, j: (i, j)),
        out_shape=jax.ShapeDtypeStruct((t, n), F32),
        compiler_params=_params("parallel", "arbitrary"),
        name="res_out",
    )(a, w, h)


def _mix_out_kernel(oa_ref, ob_ref, oc_ref, od_ref, bg_ref, w_ref, h_ref, o_ref, mn_ref):
    @pl.when(pl.program_id(1) == 0)
    def _():
        for gi, r in enumerate((oa_ref, ob_ref, oc_ref, od_ref)):
            lo = gi * GROUP_WIDTH
            mn_ref[:, lo:lo + GROUP_WIDTH] = _rms(r[...], bg_ref[gi:gi + 1, :]).astype(BF16)

    o_ref[...] = h_ref[...] + _dot(mn_ref[...], w_ref[...])


def mix_out(outs, branch_g, w, h, tm=512, tn=512):
    t, d = h.shape
    gw = GROUP_WIDTH
    grp = pl.BlockSpec((tm, gw), lambda i, j: (i, 0))
    return pl.pallas_call(
        _mix_out_kernel,
        grid=(t // tm, d // tn),
        in_specs=[grp, grp, grp, grp,
                  pl.BlockSpec((4, gw), lambda i, j: (0, 0)),
                  pl.BlockSpec((4 * gw, tn), lambda i, j: (0, j)),
                  pl.BlockSpec((tm, tn), lambda i, j: (i, j))],
        out_specs=pl.BlockSpec((tm, tn), lambda i, j: (i, j)),
        out_shape=jax.ShapeDtypeStruct((t, d), F32),
        scratch_shapes=[pltpu.VMEM((tm, 4 * gw), BF16)],
        compiler_params=_params("parallel", "arbitrary"),
        name="mix_out",
    )(*outs, branch_g, w, h)


def _final_norm_kernel(h_ref, g_ref, o_ref):
    o_ref[...] = _rms(h_ref[...], g_ref[...])


def final_rms(h, g, tm=512):
    t, d = h.shape
    return pl.pallas_call(
        _final_norm_kernel,
        grid=(t // tm,),
        in_specs=[pl.BlockSpec((tm, d), lambda i: (i, 0)),
                  pl.BlockSpec((1, d), lambda i: (0, 0))],
        out_specs=pl.BlockSpec((tm, d), lambda i: (i, 0)),
        out_shape=jax.ShapeDtypeStruct((t, d), F32),
        compiler_params=_params("parallel"),
        name="final_norm",
    )(h, g.reshape(1, d))


def _na_bias_kernel(rpb_ref, o_ref):
    h = pl.program_id(0)
    oi = pl.program_id(1)
    qc = lax.broadcasted_iota(jnp.int32, (GRID_W, GRID_W), 0)
    kc = lax.broadcasted_iota(jnp.int32, (GRID_W, GRID_W), 1)
    diff = kc - qc + (NA_COLS - 1)
    cstart = jnp.clip(qc - NA_COLS // 2, 0, GRID_W - NA_COLS)
    col_ok = (kc >= cstart) & (kc < cstart + NA_COLS)
    for kr in range(NA_ROWS):
        row = kr - oi + (NA_ROWS - 1)
        tile = jnp.zeros((GRID_W, GRID_W), F32)
        for d in range(2 * NA_COLS - 1):
            tile = jnp.where(diff == d, rpb_ref[h, row, d], tile)
        o_ref[:, kr * GRID_W:(kr + 1) * GRID_W] = jnp.where(col_ok, tile, NEG)


def na_bias(rpb):
    nh = rpb.shape[0]
    return pl.pallas_call(
        _na_bias_kernel,
        grid=(nh, NA_ROWS),
        in_specs=[pl.BlockSpec(memory_space=pltpu.SMEM)],
        out_specs=pl.BlockSpec((None, None, GRID_W, NA_ROWS * GRID_W), lambda h, o: (h, o, 0, 0)),
        out_shape=jax.ShapeDtypeStruct((nh, NA_ROWS, GRID_W, NA_ROWS * GRID_W), F32),
        compiler_params=_params("arbitrary", "arbitrary"),
        name="na_bias",
    )(rpb)


def _na_kernel(q_ref, k_ref, v_ref, bias_ref, o_ref, *, rows):
    win = NA_ROWS * GRID_W

    def body(r, carry):
        start_row = jnp.clip(r - NA_ROWS // 2, 0, rows - NA_ROWS)
        start = pl.multiple_of(start_row * GRID_W, GRID_W)
        q0 = pl.multiple_of(r * GRID_W, GRID_W)
        q = q_ref[pl.ds(q0, GRID_W), :]
        kw = k_ref[pl.ds(start, win), :]
        vw = v_ref[pl.ds(start, win), :]
        s = _dot_nt(q, kw) * SCALE + bias_ref[r - start_row]
        m = jnp.max(s, axis=-1, keepdims=True)
        p = jnp.exp(s - m)
        den = jnp.sum(p, axis=-1, keepdims=True)
        o_ref[pl.ds(q0, GRID_W), :] = _dot(p.astype(BF16), vw) / den
        return carry

    lax.fori_loop(0, rows, body, 0)


def neighbourhood_attention(u, bias):
    b, s, _ = u.shape
    rows = s // GRID_W
    col = lambda c0: pl.BlockSpec((None, s, HEAD_DIM), lambda h, n: (n, 0, c0 + h))
    return pl.pallas_call(
        functools.partial(_na_kernel, rows=rows),
        grid=(N_HEADS, b),
        in_specs=[col(COL_AQ), col(COL_AK), col(COL_AV),
                  pl.BlockSpec((None, NA_ROWS, GRID_W, NA_ROWS * GRID_W), lambda h, n: (h, 0, 0, 0))],
        out_specs=pl.BlockSpec((None, s, HEAD_DIM), lambda h, n: (n, 0, h)),
        out_shape=jax.ShapeDtypeStruct((b, s, GROUP_WIDTH), F32),
        compiler_params=_params("arbitrary", "arbitrary"),
        name="na_attn",
    )(u, u, u, bias)


def _glu(a, g):
    return a * jax.nn.sigmoid(g)


def _conv_kernel(a_ref, g_ref, ap_ref, gp_ref, an_ref, gn_ref, w_ref, b_ref, lg_ref, lb_ref,
                 o_ref, hp_ref, acc_ref, *, ts, n_tiles):
    i = pl.program_id(1)
    c = a_ref.shape[-1]
    prev_ok = (i > 0).astype(F32)
    next_ok = (i < n_tiles - 1).astype(F32)
    hp_ref[0:CONV_HALO, :] = _glu(ap_ref[...], gp_ref[...]) * prev_ok
    hp_ref[CONV_HALO:CONV_HALO + ts, :] = _glu(a_ref[...], g_ref[...])
    hp_ref[CONV_HALO + ts:2 * CONV_HALO + ts, :] = _glu(an_ref[...], gn_ref[...]) * next_ok

    shift = CONV_HALO - CONV_WIDTH // 2
    for cc in range(c // 128):
        lanes = slice(cc * 128, (cc + 1) * 128)
        acc = jnp.zeros((ts, 128), F32)
        for k in range(CONV_WIDTH):
            acc = acc + hp_ref[shift + k:shift + k + ts, lanes] * w_ref[k:k + 1, lanes]
        acc_ref[:, lanes] = acc + b_ref[:, lanes]

    hf = acc_ref[...]
    mu = jnp.mean(hf, axis=-1, keepdims=True)
    xc = hf - mu
    var = jnp.mean(xc * xc, axis=-1, keepdims=True)
    y = xc * lax.rsqrt(var + EPS) * lg_ref[...] + lb_ref[...]
    o_ref[...] = y * jax.nn.sigmoid(y)


def conformer_conv(uc, w, bias, ln_g, ln_b, ts=256):
    b, s, c2 = uc.shape
    c = c2 // 2
    n_tiles = s // ts
    hb = ts // CONV_HALO
    last = s // CONV_HALO - 1
    cur = lambda half: pl.BlockSpec((None, ts, c), lambda n, i: (n, i, half))
    prev = lambda half: pl.BlockSpec((None, CONV_HALO, c),
                                     lambda n, i: (n, jnp.maximum(i * hb - 1, 0), half))
    nxt = lambda half: pl.BlockSpec((None, CONV_HALO, c),
                                    lambda n, i: (n, jnp.minimum((i + 1) * hb, last), half))
    vec = pl.BlockSpec((1, c), lambda n, i: (0, 0))
    return pl.pallas_call(
        functools.partial(_conv_kernel, ts=ts, n_tiles=n_tiles),
        grid=(b, n_tiles),
        in_specs=[cur(0), cur(1), prev(0), prev(1), nxt(0), nxt(1),
                  pl.BlockSpec((CONV_WIDTH, c), lambda n, i: (0, 0)), vec, vec, vec],
        out_specs=pl.BlockSpec((None, ts, c), lambda n, i: (n, i, 0)),
        out_shape=jax.ShapeDtypeStruct((b, s, c), F32),
        scratch_shapes=[pltpu.VMEM((ts + 2 * CONV_HALO, c), F32), pltpu.VMEM((ts, c), F32)],
        compiler_params=_params("parallel", "arbitrary"),
        name="conformer_conv",
    )(uc, uc, uc, uc, uc, uc, w, bias.reshape(1, c), ln_g.reshape(1, c), ln_b.reshape(1, c))


def _swa_kernel(slopes_ref, sink_ref, q_ref, k_ref, v_ref, o_ref, *, seq):
    kvh = pl.program_id(1)
    blk = SWA_RADIUS
    win = 3 * blk
    group = N_HEADS // SWA_KV_HEADS
    qi = lax.broadcasted_iota(jnp.int32, (blk, win), 0)
    ki = lax.broadcasted_iota(jnp.int32, (blk, win), 1)

    def body(i, carry):
        q0 = pl.multiple_of(i * blk, blk)
        start = pl.multiple_of(jnp.clip(q0 - blk, 0, seq - win), blk)
        kw = k_ref[pl.ds(start, win), :]
        vw = v_ref[pl.ds(start, win), :]
        dist = jnp.abs(ki - qi + (start - q0))
        valid = dist <= SWA_RADIUS
        distf = dist.astype(F32)
        for g in range(group):
            head = kvh * group + g
            lanes = slice(g * HEAD_DIM, (g + 1) * HEAD_DIM)
            q = q_ref[pl.ds(q0, blk), lanes]
            s = _dot_nt(q, kw) * SCALE
            s = jnp.where(valid, s - slopes_ref[head] * distf, NEG)
            m = jnp.max(s, axis=-1, keepdims=True)
            p = jnp.exp(s - m)
            den = jnp.sum(p, axis=-1, keepdims=True)
            o = _dot(p.astype(BF16), vw) / den
            lse = m + jnp.log(den)
            o_ref[pl.ds(q0, blk), lanes] = o * jax.nn.sigmoid(lse - sink_ref[head])
        return carry

    lax.fori_loop(0, seq // blk, body, 0)


def windowed_gqa(u, slopes, sink):
    b, s, _ = u.shape
    group = N_HEADS // SWA_KV_HEADS
    qw = group * HEAD_DIM
    smem = pl.BlockSpec(memory_space=pltpu.SMEM)
    return pl.pallas_call(
        functools.partial(_swa_kernel, seq=s),
        grid=(b, SWA_KV_HEADS),
        in_specs=[smem, smem,
                  pl.BlockSpec((None, s, qw), lambda n, kv: (n, 0, COL_CQ // group + kv)),
                  pl.BlockSpec((None, s, HEAD_DIM), lambda n, kv: (n, 0, COL_CK + kv)),
                  pl.BlockSpec((None, s, HEAD_DIM), lambda n, kv: (n, 0, COL_CV + kv))],
        out_specs=pl.BlockSpec((None, s, qw), lambda n, kv: (n, 0, kv)),
        out_shape=jax.ShapeDtypeStruct((b, s, GROUP_WIDTH), F32),
        compiler_params=_params("parallel", "arbitrary"),
        name="windowed_gqa",
    )(slopes, sink, u, u, u)


DIL_TQ = 128


def _dil_kernel(slopes_ref, q_ref, k_ref, v_ref, o_ref, tb_ref, *, seq):
    h = pl.program_id(0)
    n_tiles = seq // DIL_TQ
    off = seq - DIL_TQ

    @pl.when(pl.program_id(1) == 0)
    def _():
        slope = slopes_ref[h]
        qi = lax.broadcasted_iota(jnp.int32, (DIL_TQ, 128), 0)
        li = lax.broadcasted_iota(jnp.int32, (DIL_TQ, 128), 1)

        def fill(cb, carry):
            c0 = pl.multiple_of(cb * 128, 128)
            d = li + (c0 - off) - qi
            ad = jnp.abs(d)
            cnt = jnp.zeros((DIL_TQ, 128), F32)
            for window, dil in DIL_PATTERNS:
                hit = ((d & (dil - 1)) == 0) & (ad <= window // 2)
                cnt = cnt + jnp.where(hit, 1.0, 0.0)
            bias = jnp.log(jnp.maximum(cnt, 1.0)) - slope * ad.astype(F32)
            tb_ref[:, pl.ds(c0, 128)] = jnp.where(cnt > 0.0, bias, NEG)
            return carry

        lax.fori_loop(0, (seq + off) // 128, fill, 0)

    def body(i, carry):
        q0 = pl.multiple_of(i * DIL_TQ, DIL_TQ)
        q = q_ref[pl.ds(q0, DIL_TQ), :]
        t0 = pl.multiple_of(off - q0, 128)
        s = _dot_nt(q, k_ref[...]) * SCALE + tb_ref[:, pl.ds(t0, seq)]
        m = jnp.max(s, axis=-1, keepdims=True)
        p = jnp.exp(s - m)
        den = jnp.sum(p, axis=-1, keepdims=True)
        o_ref[pl.ds(q0, DIL_TQ), :] = _dot(p.astype(BF16), v_ref[...]) / den
        return carry

    lax.fori_loop(0, n_tiles, body, 0)


def dilated_attention(u, slopes):
    b, s, _ = u.shape
    col = lambda c0: pl.BlockSpec((None, s, HEAD_DIM), lambda h, n: (n, 0, c0 + h))
    return pl.pallas_call(
        functools.partial(_dil_kernel, seq=s),
        grid=(N_HEADS, b),
        in_specs=[pl.BlockSpec(memory_space=pltpu.SMEM),
                  col(COL_DQ), col(COL_DK), col(COL_DV)],
        out_specs=pl.BlockSpec((None, s, HEAD_DIM), lambda h, n: (n, 0, h)),
        out_shape=jax.ShapeDtypeStruct((b, s, GROUP_WIDTH), F32),
        scratch_shapes=[pltpu.VMEM((DIL_TQ, 2 * s - DIL_TQ), F32)],
        compiler_params=_params("arbitrary", "arbitrary"),
        name="dilated_attn",
    )(slopes, u, u, u)


def _split_w_in(w):
    gw = GROUP_WIDTH
    attn = jnp.concatenate([w[:, :3 * gw], w[:, 5 * gw:]], axis=1)
    conv = w[:, 3 * gw:5 * gw]
    return attn.astype(BF16), conv.astype(BF16)


def kernel(x, ffn1_norm, ffn1_w_in, ffn1_w_out, mix_norm, w_in, na_rpb, conv_w, conv_b, conv_ln_g,
           conv_ln_b, swa_sink, branch_norm, w_out, ffn2_norm, ffn2_w_in, ffn2_w_out, final_norm):
    b, s, d = x.shape
    depth = w_in.shape[0]
    n_slopes = 2 * N_HEADS
    slopes = 2.0 ** (-8.0 * jnp.arange(1, n_slopes + 1, dtype=F32) / n_slopes)
    sl_c, sl_d = slopes[:N_HEADS], slopes[N_HEADS:]

    h = x.reshape(b * s, d)
    for l in range(depth):
        act = ffn_in(h, ffn1_norm[l], ffn1_w_in[l].astype(BF16))
        h = res_out(act, ffn1_w_out[l].astype(BF16), h, 0.5)

        w_attn, w_conv = _split_w_in(w_in[l])
        ua = norm_proj(h, mix_norm[l], w_attn, BF16).reshape(b, s, ATTN_COLS)
        uc = norm_proj(h, mix_norm[l], w_conv, F32).reshape(b, s, 2 * GROUP_WIDTH)
        oa = neighbourhood_attention(ua, na_bias(na_rpb[l]))
        ob = conformer_conv(uc, conv_w[l], conv_b[l], conv_ln_g[l], conv_ln_b[l])
        oc = windowed_gqa(ua, sl_c, swa_sink[l])
        od = dilated_attention(ua, sl_d)
        outs = [o.reshape(b * s, GROUP_WIDTH) for o in (oa, ob, oc, od)]
        h = mix_out(outs, branch_norm[l], w_out[l].astype(BF16), h)

        act = ffn_in(h, ffn2_norm[l], ffn2_w_in[l].astype(BF16))
        h = res_out(act, ffn2_w_out[l].astype(BF16), h, 0.5)
    return final_rms(h, final_norm).reshape(b, s, d)
```

```python
import functools

import jax
import jax.numpy as jnp
from jax import lax
from jax.experimental import pallas as pl
from jax.experimental.pallas import tpu as pltpu

D_MODEL = 4096
HEAD_DIM = 128
GROUP_WIDTH = 1024
N_HEADS = GROUP_WIDTH // HEAD_DIM
GRID_W = 64
NA_ROWS = 8
NA_COLS = 16
NA_ROWS_PER_TRIP = 8
CONV_WIDTH = 31
SUBLANES = 8
CONV_HALO = 16
SWA_KV_HEADS = 2
SWA_RADIUS = 128
SWA_BLOCKS_PER_TRIP = 2
DIL_PATTERNS = ((128, 1), (512, 4), (2048, 16))
EPS = 1e-6
NEG = -1e30
SCALE = HEAD_DIM ** -0.5

F32 = jnp.float32
BF16 = jnp.bfloat16

W_CONV_LO, W_CONV_HI = 24, 40
COL_AQ, COL_AK, COL_AV = 0, 8, 16
COL_CQ, COL_CK, COL_CV = 24, 32, 34
COL_DQ, COL_DK, COL_DV = 36, 44, 52
ATTN_COLS = 60 * HEAD_DIM

VMEM_LIMIT = 56 * 1024 * 1024


def _params(*sem):
    return pltpu.CompilerParams(dimension_semantics=sem, vmem_limit_bytes=VMEM_LIMIT)


def _dot(a, b):
    return jnp.dot(a, b, preferred_element_type=F32)


def _dot_nt(a, b):
    return lax.dot_general(a, b, (((1,), (1,)), ((), ())), preferred_element_type=F32)


def _rms(x, g):
    ms = jnp.mean(x * x, axis=-1, keepdims=True)
    return x * lax.rsqrt(ms + EPS) * g


STAT_LANES = 128


def _row_rsqrt(ss_ref, d):
    return lax.rsqrt(ss_ref[:, 0:1] / d + EPS)


def _emit_stream(y, j, o_ref, ob_ref, ss_ref):
    o_ref[...] = y
    ob_ref[...] = y.astype(BF16)
    part = jnp.broadcast_to(jnp.sum(y * y, axis=-1, keepdims=True), ss_ref.shape)

    @pl.when(j == 0)
    def _():
        ss_ref[...] = part

    @pl.when(j > 0)
    def _():
        ss_ref[...] += part


def _stream_out(t, n, tm, tn):
    specs = [pl.BlockSpec((tm, tn), lambda i, j: (i, j)),
             pl.BlockSpec((tm, tn), lambda i, j: (i, j)),
             pl.BlockSpec((tm, STAT_LANES), lambda i, j: (i, 0))]
    shapes = [jax.ShapeDtypeStruct((t, n), F32),
              jax.ShapeDtypeStruct((t, n), BF16),
              jax.ShapeDtypeStruct((t, STAT_LANES), F32)]
    return specs, shapes


def _prep_kernel(x_ref, ob_ref, ss_ref):
    x = x_ref[...]
    ob_ref[...] = x.astype(BF16)
    ss_ref[...] = jnp.broadcast_to(jnp.sum(x * x, axis=-1, keepdims=True), ss_ref.shape)


def prep(x, tm=512):
    t, d = x.shape
    return pl.pallas_call(
        _prep_kernel,
        grid=(t // tm,),
        in_specs=[pl.BlockSpec((tm, d), lambda i: (i, 0))],
        out_specs=[pl.BlockSpec((tm, d), lambda i: (i, 0)),
                   pl.BlockSpec((tm, STAT_LANES), lambda i: (i, 0))],
        out_shape=[jax.ShapeDtypeStruct((t, d), BF16),
                   jax.ShapeDtypeStruct((t, STAT_LANES), F32)],
        compiler_params=_params("parallel"),
        name="prep",
    )(x)


def _mix_in_kernel(hb_ref, ss_ref, w_ref, ua_ref, uc_ref, *, conv_lo, conv_hi):
    j = pl.program_id(1)
    y = _dot(hb_ref[...], w_ref[...]) * _row_rsqrt(ss_ref, hb_ref.shape[-1])
    is_conv = (j >= conv_lo) & (j < conv_hi)

    @pl.when(is_conv)
    def _():
        uc_ref[...] = y

    @pl.when(jnp.logical_not(is_conv))
    def _():
        ua_ref[...] = y.astype(BF16)


def mix_in(hb, ss, w, layer, tm=1024, tn=512):
    t, d = hb.shape
    n = w.shape[2]
    conv_lo = W_CONV_LO * HEAD_DIM // tn
    conv_hi = W_CONV_HI * HEAD_DIM // tn
    n_conv = conv_hi - conv_lo
    return pl.pallas_call(
        functools.partial(_mix_in_kernel, conv_lo=conv_lo, conv_hi=conv_hi),
        grid=(t // tm, n // tn),
        in_specs=[pl.BlockSpec((tm, d), lambda i, j: (i, 0)),
                  pl.BlockSpec((tm, STAT_LANES), lambda i, j: (i, 0)),
                  pl.BlockSpec((None, d, tn), lambda i, j: (layer, 0, j))],
        out_specs=[pl.BlockSpec((tm, tn), lambda i, j: (
                       i, jnp.where(j < conv_lo, j, jnp.where(j < conv_hi, conv_lo - 1, j - n_conv)))),
                   pl.BlockSpec((tm, tn), lambda i, j: (i, jnp.clip(j - conv_lo, 0, n_conv - 1)))],
        out_shape=[jax.ShapeDtypeStruct((t, n - n_conv * tn), BF16),
                   jax.ShapeDtypeStruct((t, n_conv * tn), F32)],
        compiler_params=_params("parallel", "arbitrary"),
        name="mix_in",
    )(hb, ss, w)


def _ffn_in_kernel(hb_ref, ss_ref, wg_ref, wu_ref, o_ref):
    r = _row_rsqrt(ss_ref, hb_ref.shape[-1])
    x = hb_ref[...]
    gate = _dot(x, wg_ref[...]) * r
    up = _dot(x, wu_ref[...]) * r
    o_ref[...] = (gate * jax.nn.sigmoid(gate) * up).astype(o_ref.dtype)


def ffn_in(hb, ss, w, layer, tm=1024, tf=512):
    t, d = hb.shape
    f = w.shape[2] // 2
    nf = f // tf
    return pl.pallas_call(
        _ffn_in_kernel,
        grid=(t // tm, nf),
        in_specs=[pl.BlockSpec((tm, d), lambda i, j: (i, 0)),
                  pl.BlockSpec((tm, STAT_LANES), lambda i, j: (i, 0)),
                  pl.BlockSpec((None, d, tf), lambda i, j: (layer, 0, j)),
                  pl.BlockSpec((None, d, tf), lambda i, j: (layer, 0, j + nf))],
        out_specs=pl.BlockSpec((tm, tf), lambda i, j: (i, j)),
        out_shape=jax.ShapeDtypeStruct((t, f), BF16),
        compiler_params=_params("parallel", "arbitrary"),
        name="ffn_in",
    )(hb, ss, w, w)


def _res_out_kernel(a_ref, w_ref, h_ref, o_ref, ob_ref, ss_ref, *, scale):
    y = h_ref[...] + scale * _dot(a_ref[...], w_ref[...])
    _emit_stream(y, pl.program_id(1), o_ref, ob_ref, ss_ref)


def res_out(a, w, layer, h, scale, tm=1024, tn=512):
    t, k = a.shape
    n = w.shape[2]
    out_specs, out_shape = _stream_out(t, n, tm, tn)
    return pl.pallas_call(
        functools.partial(_res_out_kernel, scale=scale),
        grid=(t // tm, n // tn),
        in_specs=[pl.BlockSpec((tm, k), lambda i, j: (i, 0)),
                  pl.BlockSpec((None, k, tn), lambda i, j: (layer, 0, j)),
                  pl.BlockSpec((tm, tn), lambda i, j: (i, j))],
        out_specs=out_specs,
        out_shape=out_shape,
        compiler_params=_params("parallel", "arbitrary"),
        name="res_out",
    )(a, w, h)


def _mix_out_kernel(oa_ref, ob_ref, oc_ref, od_ref, bg_ref, w_ref, h_ref, o_ref, hb_ref, ss_ref, mn_ref):
    j = pl.program_id(1)

    @pl.when(j == 0)
    def _():
        for gi, r in enumerate((oa_ref, ob_ref, oc_ref, od_ref)):
            lo = gi * GROUP_WIDTH
            mn_ref[:, lo:lo + GROUP_WIDTH] = _rms(r[...].astype(F32), bg_ref[gi:gi + 1, :]).astype(BF16)

    y = h_ref[...] + _dot(mn_ref[...], w_ref[...])
    _emit_stream(y, j, o_ref, hb_ref, ss_ref)


def mix_out(outs, branch_g, w, layer, h, tm=1024, tn=512):
    t, d = h.shape
    gw = GROUP_WIDTH
    grp = pl.BlockSpec((tm, gw), lambda i, j: (i, 0))
    out_specs, out_shape = _stream_out(t, d, tm, tn)
    return pl.pallas_call(
        _mix_out_kernel,
        grid=(t // tm, d // tn),
        in_specs=[grp, grp, grp, grp,
                  pl.BlockSpec((4, gw), lambda i, j: (0, 0)),
                  pl.BlockSpec((None, 4 * gw, tn), lambda i, j: (layer, 0, j)),
                  pl.BlockSpec((tm, tn), lambda i, j: (i, j))],
        out_specs=out_specs,
        out_shape=out_shape,
        scratch_shapes=[pltpu.VMEM((tm, 4 * gw), BF16)],
        compiler_params=_params("parallel", "arbitrary"),
        name="mix_out",
    )(*outs, branch_g, w, h)


def _final_norm_kernel(h_ref, g_ref, o_ref):
    o_ref[...] = _rms(h_ref[...], g_ref[...])


def final_rms(h, g, tm=512):
    t, d = h.shape
    return pl.pallas_call(
        _final_norm_kernel,
        grid=(t // tm,),
        in_specs=[pl.BlockSpec((tm, d), lambda i: (i, 0)),
                  pl.BlockSpec((1, d), lambda i: (0, 0))],
        out_specs=pl.BlockSpec((tm, d), lambda i: (i, 0)),
        out_shape=jax.ShapeDtypeStruct((t, d), F32),
        compiler_params=_params("parallel"),
        name="final_norm",
    )(h, g.reshape(1, d))


def _na_bias_kernel(rpb_ref, o_ref):
    h = pl.program_id(0)
    oi = pl.program_id(1)
    qc = lax.broadcasted_iota(jnp.int32, (GRID_W, GRID_W), 0)
    kc = lax.broadcasted_iota(jnp.int32, (GRID_W, GRID_W), 1)
    diff = kc - qc + (NA_COLS - 1)
    cstart = jnp.clip(qc - NA_COLS // 2, 0, GRID_W - NA_COLS)
    col_ok = (kc >= cstart) & (kc < cstart + NA_COLS)
    for kr in range(NA_ROWS):
        row = kr - oi + (NA_ROWS - 1)
        tile = jnp.zeros((GRID_W, GRID_W), F32)
        for d in range(2 * NA_COLS - 1):
            tile = jnp.where(diff == d, rpb_ref[h, row, d], tile)
        o_ref[:, kr * GRID_W:(kr + 1) * GRID_W] = jnp.where(col_ok, tile, NEG)


def na_bias(rpb):
    nh = rpb.shape[0]
    return pl.pallas_call(
        _na_bias_kernel,
        grid=(nh, NA_ROWS),
        in_specs=[pl.BlockSpec(memory_space=pltpu.SMEM)],
        out_specs=pl.BlockSpec((None, None, GRID_W, NA_ROWS * GRID_W), lambda h, o: (h, o, 0, 0)),
        out_shape=jax.ShapeDtypeStruct((nh, NA_ROWS, GRID_W, NA_ROWS * GRID_W), F32),
        compiler_params=_params("arbitrary", "arbitrary"),
        name="na_bias",
    )(rpb)


def _na_kernel(q_ref, k_ref, v_ref, bias_ref, o_ref, *, rows):
    win = NA_ROWS * GRID_W

    def body(g, carry):
        ss, vws, q0s = [], [], []
        for t in range(NA_ROWS_PER_TRIP):
            r = g * NA_ROWS_PER_TRIP + t
            start_row = jnp.clip(r - NA_ROWS // 2, 0, rows - NA_ROWS)
            start = pl.multiple_of(start_row * GRID_W, GRID_W)
            q0 = pl.multiple_of(r * GRID_W, GRID_W)
            q = q_ref[pl.ds(q0, GRID_W), :]
            kw = k_ref[pl.ds(start, win), :]
            vws.append(v_ref[pl.ds(start, win), :])
            q0s.append(q0)
            ss.append(_dot_nt(q, kw) * SCALE + bias_ref[r - start_row])
        ps, dens = [], []
        for s in ss:
            m = jnp.max(s, axis=-1, keepdims=True)
            p = jnp.exp(s - m)
            dens.append(jnp.sum(p, axis=-1, keepdims=True))
            ps.append(p.astype(BF16))
        for q0, p, vw, den in zip(q0s, ps, vws, dens):
            o_ref[pl.ds(q0, GRID_W), :] = (_dot(p, vw) / den).astype(o_ref.dtype)
        return carry

    lax.fori_loop(0, rows // NA_ROWS_PER_TRIP, body, 0)


def neighbourhood_attention(u, bias):
    b, s, _ = u.shape
    rows = s // GRID_W
    col = lambda c0: pl.BlockSpec((None, s, HEAD_DIM), lambda h, n: (n, 0, c0 + h))
    return pl.pallas_call(
        functools.partial(_na_kernel, rows=rows),
        grid=(N_HEADS, b),
        in_specs=[col(COL_AQ), col(COL_AK), col(COL_AV),
                  pl.BlockSpec((None, NA_ROWS, GRID_W, NA_ROWS * GRID_W), lambda h, n: (h, 0, 0, 0))],
        out_specs=pl.BlockSpec((None, s, HEAD_DIM), lambda h, n: (n, 0, h)),
        out_shape=jax.ShapeDtypeStruct((b, s, GROUP_WIDTH), BF16),
        compiler_params=_params("arbitrary", "arbitrary"),
        name="na_attn",
    )(u, u, u, bias)


def _glu(a, g):
    return a * jax.nn.sigmoid(g)


def _conv_kernel(a_ref, g_ref, ap_ref, gp_ref, an_ref, gn_ref, w_ref, b_ref, lg_ref, lb_ref,
                 o_ref, hp_ref, acc_ref, hs_ref, *, ts, n_tiles):
    i = pl.program_id(1)
    c = a_ref.shape[-1]
    prev_ok = (i > 0).astype(F32)
    next_ok = (i < n_tiles - 1).astype(F32)
    hp_ref[0:CONV_HALO, :] = _glu(ap_ref[...], gp_ref[...]) * prev_ok
    hp_ref[CONV_HALO:CONV_HALO + ts, :] = _glu(a_ref[...], g_ref[...])
    hp_ref[CONV_HALO + ts:2 * CONV_HALO + ts, :] = _glu(an_ref[...], gn_ref[...]) * next_ok

    shift = CONV_HALO - CONV_WIDTH // 2
    slab = ts + 2 * CONV_HALO - SUBLANES
    for rem in range(1, SUBLANES):
        hs_ref[rem - 1] = hp_ref[rem:rem + slab, :]
    for cc in range(c // 128):
        lanes = slice(cc * 128, (cc + 1) * 128)
        acc = jnp.zeros((ts, 128), F32)
        for rem in range(SUBLANES):
            for base in range(0, slab - ts + 1, SUBLANES):
                k = base + rem - shift
                if 0 <= k < CONV_WIDTH:
                    if rem == 0:
                        rows = hp_ref[base:base + ts, lanes]
                    else:
                        rows = hs_ref[rem - 1, base:base + ts, lanes]
                    acc = acc + rows * w_ref[k:k + 1, lanes]
        acc_ref[:, lanes] = acc + b_ref[:, lanes]

    hf = acc_ref[...]
    mu = jnp.mean(hf, axis=-1, keepdims=True)
    xc = hf - mu
    var = jnp.mean(xc * xc, axis=-1, keepdims=True)
    y = xc * lax.rsqrt(var + EPS) * lg_ref[...] + lb_ref[...]
    o_ref[...] = (y * jax.nn.sigmoid(y)).astype(o_ref.dtype)


def conformer_conv(uc, w, bias, ln_g, ln_b, ts=256):
    b, s, c2 = uc.shape
    c = c2 // 2
    n_tiles = s // ts
    hb = ts // CONV_HALO
    last = s // CONV_HALO - 1
    cur = lambda half: pl.BlockSpec((None, ts, c), lambda n, i: (n, i, half))
    prev = lambda half: pl.BlockSpec((None, CONV_HALO, c),
                                     lambda n, i: (n, jnp.maximum(i * hb - 1, 0), half))
    nxt = lambda half: pl.BlockSpec((None, CONV_HALO, c),
                                    lambda n, i: (n, jnp.minimum((i + 1) * hb, last), half))
    vec = pl.BlockSpec((1, c), lambda n, i: (0, 0))
    return pl.pallas_call(
        functools.partial(_conv_kernel, ts=ts, n_tiles=n_tiles),
        grid=(b, n_tiles),
        in_specs=[cur(0), cur(1), prev(0), prev(1), nxt(0), nxt(1),
                  pl.BlockSpec((CONV_WIDTH, c), lambda n, i: (0, 0)), vec, vec, vec],
        out_specs=pl.BlockSpec((None, ts, c), lambda n, i: (n, i, 0)),
        out_shape=jax.ShapeDtypeStruct((b, s, c), BF16),
        scratch_shapes=[pltpu.VMEM((ts + 2 * CONV_HALO, c), F32), pltpu.VMEM((ts, c), F32),
                        pltpu.VMEM((SUBLANES - 1, ts + 2 * CONV_HALO - SUBLANES, c), F32)],
        compiler_params=_params("parallel", "arbitrary"),
        name="conformer_conv",
    )(uc, uc, uc, uc, uc, uc, w, bias.reshape(1, c), ln_g.reshape(1, c), ln_b.reshape(1, c))


def _swa_kernel(slopes_ref, sink_ref, q_ref, k_ref, v_ref, o_ref, *, seq):
    kvh = pl.program_id(1)
    blk = SWA_RADIUS
    win = 3 * blk
    group = N_HEADS // SWA_KV_HEADS
    qi = lax.broadcasted_iota(jnp.int32, (blk, win), 0)
    ki = lax.broadcasted_iota(jnp.int32, (blk, win), 1)

    def body(it, carry):
        chains = []
        for t in range(SWA_BLOCKS_PER_TRIP):
            q0 = pl.multiple_of((it * SWA_BLOCKS_PER_TRIP + t) * blk, blk)
            start = pl.multiple_of(jnp.clip(q0 - blk, 0, seq - win), blk)
            kw = k_ref[pl.ds(start, win), :]
            vw = v_ref[pl.ds(start, win), :]
            q4 = q_ref[pl.ds(q0, blk), :]
            qs = jnp.concatenate([q4[:, g * HEAD_DIM:(g + 1) * HEAD_DIM] for g in range(group)], axis=0)
            chains.append((q0, start, vw, _dot_nt(qs, kw) * SCALE))
        outs = []
        for q0, start, vw, s_all in chains:
            dist = jnp.abs(ki - qi + (start - q0))
            valid = dist <= SWA_RADIUS
            distf = dist.astype(F32)
            ps, gates = [], []
            for g in range(group):
                head = kvh * group + g
                s = s_all[g * blk:(g + 1) * blk]
                s = jnp.where(valid, s - slopes_ref[head] * distf, NEG)
                m = jnp.max(s, axis=-1, keepdims=True)
                p = jnp.exp(s - m)
                den = jnp.sum(p, axis=-1, keepdims=True)
                lse = m + jnp.log(den)
                gates.append(jax.nn.sigmoid(lse - sink_ref[head]) / den)
                ps.append(p.astype(BF16))
            outs.append((q0, vw, jnp.concatenate(ps, axis=0), gates))
        for q0, vw, p_all, gates in outs:
            o_all = _dot(p_all, vw)
            for g in range(group):
                o = o_all[g * blk:(g + 1) * blk] * gates[g]
                o_ref[pl.ds(q0, blk), g * HEAD_DIM:(g + 1) * HEAD_DIM] = o.astype(o_ref.dtype)
        return carry

    lax.fori_loop(0, seq // (blk * SWA_BLOCKS_PER_TRIP), body, 0)


def windowed_gqa(u, slopes, sink):
    b, s, _ = u.shape
    group = N_HEADS // SWA_KV_HEADS
    qw = group * HEAD_DIM
    smem = pl.BlockSpec(memory_space=pltpu.SMEM)
    return pl.pallas_call(
        functools.partial(_swa_kernel, seq=s),
        grid=(b, SWA_KV_HEADS),
        in_specs=[smem, smem,
                  pl.BlockSpec((None, s, qw), lambda n, kv: (n, 0, COL_CQ // group + kv)),
                  pl.BlockSpec((None, s, HEAD_DIM), lambda n, kv: (n, 0, COL_CK + kv)),
                  pl.BlockSpec((None, s, HEAD_DIM), lambda n, kv: (n, 0, COL_CV + kv))],
        out_specs=pl.BlockSpec((None, s, qw), lambda n, kv: (n, 0, kv)),
        out_shape=jax.ShapeDtypeStruct((b, s, GROUP_WIDTH), BF16),
        compiler_params=_params("parallel", "arbitrary"),
        name="windowed_gqa",
    )(slopes, sink, u, u, u)


DIL_TQ = 128
DIL_TILES_PER_TRIP = 2


def _dil_kernel(slopes_ref, q_ref, k_ref, v_ref, o_ref, tb_ref, *, seq):
    h = pl.program_id(0)
    n_tiles = seq // DIL_TQ
    off = seq - DIL_TQ

    @pl.when(pl.program_id(1) == 0)
    def _():
        slope = slopes_ref[h]
        qi = lax.broadcasted_iota(jnp.int32, (DIL_TQ, 128), 0)
        li = lax.broadcasted_iota(jnp.int32, (DIL_TQ, 128), 1)

        def fill(cb, carry):
            c0 = pl.multiple_of(cb * 128, 128)
            d = li + (c0 - off) - qi
            ad = jnp.abs(d)
            cnt = jnp.zeros((DIL_TQ, 128), F32)
            for window, dil in DIL_PATTERNS:
                hit = ((d & (dil - 1)) == 0) & (ad <= window // 2)
                cnt = cnt + jnp.where(hit, 1.0, 0.0)
            bias = jnp.log(jnp.maximum(cnt, 1.0)) - slope * ad.astype(F32)
            tb_ref[:, pl.ds(c0, 128)] = jnp.where(cnt > 0.0, bias, NEG)
            return carry

        lax.fori_loop(0, (seq + off) // 128, fill, 0)

    def body(it, carry):
        q0s, ss = [], []
        for t in range(DIL_TILES_PER_TRIP):
            q0 = pl.multiple_of((it * DIL_TILES_PER_TRIP + t) * DIL_TQ, DIL_TQ)
            q = q_ref[pl.ds(q0, DIL_TQ), :]
            t0 = pl.multiple_of(off - q0, 128)
            q0s.append(q0)
            ss.append(_dot_nt(q, k_ref[...]) * SCALE + tb_ref[:, pl.ds(t0, seq)])
        ps, dens = [], []
        for s in ss:
            m = jnp.max(s, axis=-1, keepdims=True)
            p = jnp.exp(s - m)
            dens.append(jnp.sum(p, axis=-1, keepdims=True))
            ps.append(p.astype(BF16))
        for q0, p, den in zip(q0s, ps, dens):
            o_ref[pl.ds(q0, DIL_TQ), :] = (_dot(p, v_ref[...]) / den).astype(o_ref.dtype)
        return carry

    lax.fori_loop(0, n_tiles // DIL_TILES_PER_TRIP, body, 0)


def dilated_attention(u, slopes):
    b, s, _ = u.shape
    col = lambda c0: pl.BlockSpec((None, s, HEAD_DIM), lambda h, n: (n, 0, c0 + h))
    return pl.pallas_call(
        functools.partial(_dil_kernel, seq=s),
        grid=(N_HEADS, b),
        in_specs=[pl.BlockSpec(memory_space=pltpu.SMEM),
                  col(COL_DQ), col(COL_DK), col(COL_DV)],
        out_specs=pl.BlockSpec((None, s, HEAD_DIM), lambda h, n: (n, 0, h)),
        out_shape=jax.ShapeDtypeStruct((b, s, GROUP_WIDTH), BF16),
        scratch_shapes=[pltpu.VMEM((DIL_TQ, 2 * s - DIL_TQ), F32)],
        compiler_params=_params("arbitrary", "arbitrary"),
        name="dilated_attn",
    )(slopes, u, u, u)


def kernel(x, ffn1_norm, ffn1_w_in, ffn1_w_out, mix_norm, w_in, na_rpb, conv_w, conv_b, conv_ln_g,
           conv_ln_b, swa_sink, branch_norm, w_out, ffn2_norm, ffn2_w_in, ffn2_w_out, final_norm):
    b, s, d = x.shape
    depth = w_in.shape[0]
    n_slopes = 2 * N_HEADS
    slopes = 2.0 ** (-8.0 * jnp.arange(1, n_slopes + 1, dtype=F32) / n_slopes)
    sl_c, sl_d = slopes[:N_HEADS], slopes[N_HEADS:]

    fold = lambda w, g: (w * g[:, :, None]).astype(BF16)
    ffn1_wi, ffn1_wo = fold(ffn1_w_in, ffn1_norm), ffn1_w_out.astype(BF16)
    ffn2_wi, ffn2_wo = fold(ffn2_w_in, ffn2_norm), ffn2_w_out.astype(BF16)
    mix_wi, mix_wo = fold(w_in, mix_norm), w_out.astype(BF16)

    h = x.reshape(b * s, d)
    hb, ss = prep(h)
    for l in range(depth):
        act = ffn_in(hb, ss, ffn1_wi, l)
        h, hb, ss = res_out(act, ffn1_wo, l, h, 0.5)

        ua, uc = mix_in(hb, ss, mix_wi, l)
        ua = ua.reshape(b, s, ATTN_COLS)
        uc = uc.reshape(b, s, 2 * GROUP_WIDTH)
        oa = neighbourhood_attention(ua, na_bias(na_rpb[l]))
        ob = conformer_conv(uc, conv_w[l], conv_b[l], conv_ln_g[l], conv_ln_b[l])
        oc = windowed_gqa(ua, sl_c, swa_sink[l])
        od = dilated_attention(ua, sl_d)
        outs = [o.reshape(b * s, GROUP_WIDTH) for o in (oa, ob, oc, od)]
        h, hb, ss = mix_out(outs, branch_norm[l], mix_wo, l, h)

        act = ffn_in(hb, ss, ffn2_wi, l)
        h, hb, ss = res_out(act, ffn2_wo, l, h, 0.5)
    return final_rms(h, final_norm).reshape(b, s, d)
```

```python
import functools

import jax
import jax.numpy as jnp
from jax import lax
from jax.experimental import pallas as pl
from jax.experimental.pallas import tpu as pltpu

D_MODEL = 4096
HEAD_DIM = 128
GROUP_WIDTH = 1024
N_HEADS = GROUP_WIDTH // HEAD_DIM
GRID_W = 64
NA_ROWS = 8
NA_COLS = 16
NA_ROWS_PER_TRIP = 8
CONV_WIDTH = 31
SUBLANES = 8
CONV_HALO = 16
SWA_KV_HEADS = 2
SWA_RADIUS = 128
SWA_BLOCKS_PER_TRIP = 2
DIL_PATTERNS = ((128, 1), (512, 4), (2048, 16))
EPS = 1e-6
NEG = -1e30
SCALE = HEAD_DIM ** -0.5

F32 = jnp.float32
BF16 = jnp.bfloat16

W_CONV_LO, W_CONV_HI = 24, 40
COL_AQ, COL_AK, COL_AV = 0, 8, 16
COL_CQ, COL_CK, COL_CV = 24, 32, 34
COL_DQ, COL_DK, COL_DV = 36, 44, 52
ATTN_COLS = 60 * HEAD_DIM

VMEM_LIMIT = 56 * 1024 * 1024


def _params(*sem):
    return pltpu.CompilerParams(dimension_semantics=sem, vmem_limit_bytes=VMEM_LIMIT)


def _dot(a, b):
    return jnp.dot(a, b, preferred_element_type=F32)


def _dot_nt(a, b):
    return lax.dot_general(a, b, (((1,), (1,)), ((), ())), preferred_element_type=F32)


def _rms(x, g):
    ms = jnp.mean(x * x, axis=-1, keepdims=True)
    return x * lax.rsqrt(ms + EPS) * g


STAT_LANES = 128


def _row_rsqrt(ss_ref, d):
    return lax.rsqrt(ss_ref[:, 0:1] / d + EPS)


def _emit_stream(y, j, o_ref, ob_ref, ss_ref):
    o_ref[...] = y
    ob_ref[...] = y.astype(BF16)
    part = jnp.broadcast_to(jnp.sum(y * y, axis=-1, keepdims=True), ss_ref.shape)

    @pl.when(j == 0)
    def _():
        ss_ref[...] = part

    @pl.when(j > 0)
    def _():
        ss_ref[...] += part


def _stream_out(t, n, tm, tn):
    specs = [pl.BlockSpec((tm, tn), lambda i, j: (i, j)),
             pl.BlockSpec((tm, tn), lambda i, j: (i, j)),
             pl.BlockSpec((tm, STAT_LANES), lambda i, j: (i, 0))]
    shapes = [jax.ShapeDtypeStruct((t, n), F32),
              jax.ShapeDtypeStruct((t, n), BF16),
              jax.ShapeDtypeStruct((t, STAT_LANES), F32)]
    return specs, shapes


def _prep_kernel(x_ref, ob_ref, ss_ref):
    x = x_ref[...]
    ob_ref[...] = x.astype(BF16)
    ss_ref[...] = jnp.broadcast_to(jnp.sum(x * x, axis=-1, keepdims=True), ss_ref.shape)


def prep(x, tm=512):
    t, d = x.shape
    return pl.pallas_call(
        _prep_kernel,
        grid=(t // tm,),
        in_specs=[pl.BlockSpec((tm, d), lambda i: (i, 0))],
        out_specs=[pl.BlockSpec((tm, d), lambda i: (i, 0)),
                   pl.BlockSpec((tm, STAT_LANES), lambda i: (i, 0))],
        out_shape=[jax.ShapeDtypeStruct((t, d), BF16),
                   jax.ShapeDtypeStruct((t, STAT_LANES), F32)],
        compiler_params=_params("parallel"),
        name="prep",
    )(x)


def _mix_in_kernel(hb_ref, ss_ref, w_ref, o_ref):
    y = _dot(hb_ref[...], w_ref[...]) * _row_rsqrt(ss_ref, hb_ref.shape[-1])
    o_ref[...] = y.astype(o_ref.dtype)


def _mix_in_call(hb, ss, w, layer, w_tile, n_out, out_dtype, tm, tn):
    t, d = hb.shape
    return pl.pallas_call(
        _mix_in_kernel,
        grid=(t // tm, n_out // tn),
        in_specs=[pl.BlockSpec((tm, d), lambda i, j: (i, 0)),
                  pl.BlockSpec((tm, STAT_LANES), lambda i, j: (i, 0)),
                  pl.BlockSpec((None, d, tn), lambda i, j: (layer, 0, w_tile(j)))],
        out_specs=pl.BlockSpec((tm, tn), lambda i, j: (i, j)),
        out_shape=jax.ShapeDtypeStruct((t, n_out), out_dtype),
        compiler_params=_params("parallel", "arbitrary"),
        name="mix_in",
    )(hb, ss, w)


def mix_in(hb, ss, w, layer, tm=1024, tn=512):
    n = w.shape[2]
    conv_lo = W_CONV_LO * HEAD_DIM // tn
    n_conv = (W_CONV_HI - W_CONV_LO) * HEAD_DIM // tn
    skip_conv = lambda j: jnp.where(j < conv_lo, j, j + n_conv)
    ua = _mix_in_call(hb, ss, w, layer, skip_conv, n - n_conv * tn, BF16, tm, tn)
    uc = _mix_in_call(hb, ss, w, layer, lambda j: j + conv_lo, n_conv * tn, F32, tm, tn)
    return ua, uc


def _ffn_in_kernel(hb_ref, ss_ref, wg_ref, wu_ref, o_ref):
    r = _row_rsqrt(ss_ref, hb_ref.shape[-1])
    x = hb_ref[...]
    gate = _dot(x, wg_ref[...]) * r
    up = _dot(x, wu_ref[...]) * r
    o_ref[...] = (gate * jax.nn.sigmoid(gate) * up).astype(o_ref.dtype)


def ffn_in(hb, ss, w, layer, tm=1024, tf=512):
    t, d = hb.shape
    f = w.shape[2] // 2
    nf = f // tf
    return pl.pallas_call(
        _ffn_in_kernel,
        grid=(t // tm, nf),
        in_specs=[pl.BlockSpec((tm, d), lambda i, j: (i, 0)),
                  pl.BlockSpec((tm, STAT_LANES), lambda i, j: (i, 0)),
                  pl.BlockSpec((None, d, tf), lambda i, j: (layer, 0, j)),
                  pl.BlockSpec((None, d, tf), lambda i, j: (layer, 0, j + nf))],
        out_specs=pl.BlockSpec((tm, tf), lambda i, j: (i, j)),
        out_shape=jax.ShapeDtypeStruct((t, f), BF16),
        compiler_params=_params("parallel", "arbitrary"),
        name="ffn_in",
    )(hb, ss, w, w)


def _res_out_kernel(a_ref, w_ref, h_ref, o_ref, ob_ref, ss_ref, *, scale):
    y = h_ref[...] + scale * _dot(a_ref[...], w_ref[...])
    _emit_stream(y, pl.program_id(1), o_ref, ob_ref, ss_ref)


def res_out(a, w, layer, h, scale, tm=1024, tn=512):
    t, k = a.shape
    n = w.shape[2]
    out_specs, out_shape = _stream_out(t, n, tm, tn)
    return pl.pallas_call(
        functools.partial(_res_out_kernel, scale=scale),
        grid=(t // tm, n // tn),
        in_specs=[pl.BlockSpec((tm, k), lambda i, j: (i, 0)),
                  pl.BlockSpec((None, k, tn), lambda i, j: (layer, 0, j)),
                  pl.BlockSpec((tm, tn), lambda i, j: (i, j))],
        out_specs=out_specs,
        out_shape=out_shape,
        compiler_params=_params("parallel", "arbitrary"),
        name="res_out",
    )(a, w, h)


def _mix_out_kernel(oa_ref, ob_ref, oc_ref, od_ref, w_ref, h_ref, o_ref, hb_ref, ss_ref, rs_ref):
    j = pl.program_id(1)
    groups = (oa_ref, ob_ref, oc_ref, od_ref)

    @pl.when(j == 0)
    def _():
        for gi, r in enumerate(groups):
            x = r[...].astype(F32)
            ms = jnp.mean(x * x, axis=-1, keepdims=True)
            rs_ref[gi] = jnp.broadcast_to(lax.rsqrt(ms + EPS), rs_ref.shape[1:])

    y = h_ref[...]
    for gi, r in enumerate(groups):
        lo = gi * GROUP_WIDTH
        y = y + rs_ref[gi][:, 0:1] * _dot(r[...], w_ref[lo:lo + GROUP_WIDTH, :])
    _emit_stream(y, j, o_ref, hb_ref, ss_ref)


def mix_out(outs, w, layer, h, tm=1024, tn=512):
    t, d = h.shape
    gw = GROUP_WIDTH
    grp = pl.BlockSpec((tm, gw), lambda i, j: (i, 0))
    out_specs, out_shape = _stream_out(t, d, tm, tn)
    return pl.pallas_call(
        _mix_out_kernel,
        grid=(t // tm, d // tn),
        in_specs=[grp, grp, grp, grp,
                  pl.BlockSpec((None, len(outs) * gw, tn), lambda i, j: (layer, 0, j)),
                  pl.BlockSpec((tm, tn), lambda i, j: (i, j))],
        out_specs=out_specs,
        out_shape=out_shape,
        scratch_shapes=[pltpu.VMEM((len(outs), tm, STAT_LANES), F32)],
        compiler_params=_params("parallel", "arbitrary"),
        name="mix_out",
    )(*outs, w, h)


def _final_norm_kernel(h_ref, g_ref, o_ref):
    o_ref[...] = _rms(h_ref[...], g_ref[...])


def final_rms(h, g, tm=512):
    t, d = h.shape
    return pl.pallas_call(
        _final_norm_kernel,
        grid=(t // tm,),
        in_specs=[pl.BlockSpec((tm, d), lambda i: (i, 0)),
                  pl.BlockSpec((1, d), lambda i: (0, 0))],
        out_specs=pl.BlockSpec((tm, d), lambda i: (i, 0)),
        out_shape=jax.ShapeDtypeStruct((t, d), F32),
        compiler_params=_params("parallel"),
        name="final_norm",
    )(h, g.reshape(1, d))


def _na_bias_kernel(rpb_ref, o_ref):
    h = pl.program_id(0)
    oi = pl.program_id(1)
    qc = lax.broadcasted_iota(jnp.int32, (GRID_W, GRID_W), 0)
    kc = lax.broadcasted_iota(jnp.int32, (GRID_W, GRID_W), 1)
    diff = kc - qc + (NA_COLS - 1)
    cstart = jnp.clip(qc - NA_COLS // 2, 0, GRID_W - NA_COLS)
    col_ok = (kc >= cstart) & (kc < cstart + NA_COLS)
    for kr in range(NA_ROWS):
        row = kr - oi + (NA_ROWS - 1)
        tile = jnp.zeros((GRID_W, GRID_W), F32)
        for d in range(2 * NA_COLS - 1):
            tile = jnp.where(diff == d, rpb_ref[h, row, d], tile)
        o_ref[:, kr * GRID_W:(kr + 1) * GRID_W] = jnp.where(col_ok, tile, NEG)


def na_bias(rpb):
    nh = rpb.shape[0]
    return pl.pallas_call(
        _na_bias_kernel,
        grid=(nh, NA_ROWS),
        in_specs=[pl.BlockSpec(memory_space=pltpu.SMEM)],
        out_specs=pl.BlockSpec((None, None, GRID_W, NA_ROWS * GRID_W), lambda h, o: (h, o, 0, 0)),
        out_shape=jax.ShapeDtypeStruct((nh, NA_ROWS, GRID_W, NA_ROWS * GRID_W), F32),
        compiler_params=_params("arbitrary", "arbitrary"),
        name="na_bias",
    )(rpb)


def _na_kernel(q_ref, k_ref, v_ref, bias_ref, o_ref, *, rows):
    win = NA_ROWS * GRID_W

    def body(g, carry):
        ss, vws, q0s = [], [], []
        for t in range(NA_ROWS_PER_TRIP):
            r = g * NA_ROWS_PER_TRIP + t
            start_row = jnp.clip(r - NA_ROWS // 2, 0, rows - NA_ROWS)
            start = pl.multiple_of(start_row * GRID_W, GRID_W)
            q0 = pl.multiple_of(r * GRID_W, GRID_W)
            q = q_ref[pl.ds(q0, GRID_W), :]
            kw = k_ref[pl.ds(start, win), :]
            vws.append(v_ref[pl.ds(start, win), :])
            q0s.append(q0)
            ss.append(_dot_nt(q, kw) * SCALE + bias_ref[r - start_row])
        ps, dens = [], []
        for s in ss:
            m = jnp.max(s, axis=-1, keepdims=True)
            p = jnp.exp(s - m)
            dens.append(jnp.sum(p, axis=-1, keepdims=True))
            ps.append(p.astype(BF16))
        for q0, p, vw, den in zip(q0s, ps, vws, dens):
            o_ref[pl.ds(q0, GRID_W), :] = (_dot(p, vw) / den).astype(o_ref.dtype)
        return carry

    lax.fori_loop(0, rows // NA_ROWS_PER_TRIP, body, 0)


def neighbourhood_attention(u, bias):
    b, s, _ = u.shape
    rows = s // GRID_W
    col = lambda c0: pl.BlockSpec((None, s, HEAD_DIM), lambda h, n: (n, 0, c0 + h))
    return pl.pallas_call(
        functools.partial(_na_kernel, rows=rows),
        grid=(N_HEADS, b),
        in_specs=[col(COL_AQ), col(COL_AK), col(COL_AV),
                  pl.BlockSpec((None, NA_ROWS, GRID_W, NA_ROWS * GRID_W), lambda h, n: (h, 0, 0, 0))],
        out_specs=pl.BlockSpec((None, s, HEAD_DIM), lambda h, n: (n, 0, h)),
        out_shape=jax.ShapeDtypeStruct((b, s, GROUP_WIDTH), BF16),
        compiler_params=_params("arbitrary", "arbitrary"),
        name="na_attn",
    )(u, u, u, bias)


def _glu(a, g):
    return a * jax.nn.sigmoid(g)


def _conv_kernel(a_ref, g_ref, ap_ref, gp_ref, an_ref, gn_ref, w_ref, b_ref, lg_ref, lb_ref,
                 o_ref, hp_ref, acc_ref, hs_ref, *, ts, n_tiles):
    i = pl.program_id(1)
    c = a_ref.shape[-1]
    prev_ok = (i > 0).astype(F32)
    next_ok = (i < n_tiles - 1).astype(F32)
    hp_ref[0:CONV_HALO, :] = _glu(ap_ref[...], gp_ref[...]) * prev_ok
    hp_ref[CONV_HALO:CONV_HALO + ts, :] = _glu(a_ref[...], g_ref[...])
    hp_ref[CONV_HALO + ts:2 * CONV_HALO + ts, :] = _glu(an_ref[...], gn_ref[...]) * next_ok

    shift = CONV_HALO - CONV_WIDTH // 2
    slab = ts + 2 * CONV_HALO - SUBLANES
    for rem in range(1, SUBLANES):
        hs_ref[rem - 1] = hp_ref[rem:rem + slab, :]
    for cc in range(c // 128):
        lanes = slice(cc * 128, (cc + 1) * 128)
        acc = jnp.zeros((ts, 128), F32)
        for rem in range(SUBLANES):
            for base in range(0, slab - ts + 1, SUBLANES):
                k = base + rem - shift
                if 0 <= k < CONV_WIDTH:
                    if rem == 0:
                        rows = hp_ref[base:base + ts, lanes]
                    else:
                        rows = hs_ref[rem - 1, base:base + ts, lanes]
                    acc = acc + rows * w_ref[k:k + 1, lanes]
        acc_ref[:, lanes] = acc + b_ref[:, lanes]

    hf = acc_ref[...]
    mu = jnp.mean(hf, axis=-1, keepdims=True)
    xc = hf - mu
    var = jnp.mean(xc * xc, axis=-1, keepdims=True)
    y = xc * lax.rsqrt(var + EPS) * lg_ref[...] + lb_ref[...]
    o_ref[...] = (y * jax.nn.sigmoid(y)).astype(o_ref.dtype)


def conformer_conv(uc, w, bias, ln_g, ln_b, ts=256):
    b, s, c2 = uc.shape
    c = c2 // 2
    n_tiles = s // ts
    hb = ts // CONV_HALO
    last = s // CONV_HALO - 1
    cur = lambda half: pl.BlockSpec((None, ts, c), lambda n, i: (n, i, half))
    prev = lambda half: pl.BlockSpec((None, CONV_HALO, c),
                                     lambda n, i: (n, jnp.maximum(i * hb - 1, 0), half))
    nxt = lambda half: pl.BlockSpec((None, CONV_HALO, c),
                                    lambda n, i: (n, jnp.minimum((i + 1) * hb, last), half))
    vec = pl.BlockSpec((1, c), lambda n, i: (0, 0))
    return pl.pallas_call(
        functools.partial(_conv_kernel, ts=ts, n_tiles=n_tiles),
        grid=(b, n_tiles),
        in_specs=[cur(0), cur(1), prev(0), prev(1), nxt(0), nxt(1),
                  pl.BlockSpec((CONV_WIDTH, c), lambda n, i: (0, 0)), vec, vec, vec],
        out_specs=pl.BlockSpec((None, ts, c), lambda n, i: (n, i, 0)),
        out_shape=jax.ShapeDtypeStruct((b, s, c), BF16),
        scratch_shapes=[pltpu.VMEM((ts + 2 * CONV_HALO, c), F32), pltpu.VMEM((ts, c), F32),
                        pltpu.VMEM((SUBLANES - 1, ts + 2 * CONV_HALO - SUBLANES, c), F32)],
        compiler_params=_params("parallel", "arbitrary"),
        name="conformer_conv",
    )(uc, uc, uc, uc, uc, uc, w, bias.reshape(1, c), ln_g.reshape(1, c), ln_b.reshape(1, c))


def _swa_kernel(slopes_ref, sink_ref, q_ref, k_ref, v_ref, o_ref, *, seq):
    kvh = pl.program_id(1)
    blk = SWA_RADIUS
    win = 3 * blk
    group = N_HEADS // SWA_KV_HEADS
    qi = lax.broadcasted_iota(jnp.int32, (blk, win), 0)
    ki = lax.broadcasted_iota(jnp.int32, (blk, win), 1)

    def body(it, carry):
        chains = []
        for t in range(SWA_BLOCKS_PER_TRIP):
            q0 = pl.multiple_of((it * SWA_BLOCKS_PER_TRIP + t) * blk, blk)
            start = pl.multiple_of(jnp.clip(q0 - blk, 0, seq - win), blk)
            kw = k_ref[pl.ds(start, win), :]
            vw = v_ref[pl.ds(start, win), :]
            q4 = q_ref[pl.ds(q0, blk), :]
            qs = jnp.concatenate([q4[:, g * HEAD_DIM:(g + 1) * HEAD_DIM] for g in range(group)], axis=0)
            chains.append((q0, start, vw, _dot_nt(qs, kw) * SCALE))
        outs = []
        for q0, start, vw, s_all in chains:
            dist = jnp.abs(ki - qi + (start - q0))
            valid = dist <= SWA_RADIUS
            distf = dist.astype(F32)
            ps, gates = [], []
            for g in range(group):
                head = kvh * group + g
                s = s_all[g * blk:(g + 1) * blk]
                s = jnp.where(valid, s - slopes_ref[head] * distf, NEG)
                m = jnp.max(s, axis=-1, keepdims=True)
                p = jnp.exp(s - m)
                den = jnp.sum(p, axis=-1, keepdims=True)
                lse = m + jnp.log(den)
                gates.append(jax.nn.sigmoid(lse - sink_ref[head]) / den)
                ps.append(p.astype(BF16))
            outs.append((q0, vw, jnp.concatenate(ps, axis=0), gates))
        for q0, vw, p_all, gates in outs:
            o_all = _dot(p_all, vw)
            for g in range(group):
                o = o_all[g * blk:(g + 1) * blk] * gates[g]
                o_ref[pl.ds(q0, blk), g * HEAD_DIM:(g + 1) * HEAD_DIM] = o.astype(o_ref.dtype)
        return carry

    lax.fori_loop(0, seq // (blk * SWA_BLOCKS_PER_TRIP), body, 0)


def windowed_gqa(u, slopes, sink):
    b, s, _ = u.shape
    group = N_HEADS // SWA_KV_HEADS
    qw = group * HEAD_DIM
    smem = pl.BlockSpec(memory_space=pltpu.SMEM)
    return pl.pallas_call(
        functools.partial(_swa_kernel, seq=s),
        grid=(b, SWA_KV_HEADS),
        in_specs=[smem, smem,
                  pl.BlockSpec((None, s, qw), lambda n, kv: (n, 0, COL_CQ // group + kv)),
                  pl.BlockSpec((None, s, HEAD_DIM), lambda n, kv: (n, 0, COL_CK + kv)),
                  pl.BlockSpec((None, s, HEAD_DIM), lambda n, kv: (n, 0, COL_CV + kv))],
        out_specs=pl.BlockSpec((None, s, qw), lambda n, kv: (n, 0, kv)),
        out_shape=jax.ShapeDtypeStruct((b, s, GROUP_WIDTH), BF16),
        compiler_params=_params("parallel", "arbitrary"),
        name="windowed_gqa",
    )(slopes, sink, u, u, u)


DIL_TQ = 128
DIL_RES = 4
DIL_NEAR = 160
DIL_NEAR_PAD = 16
DIL_TILES_PER_TRIP = 2


def _dil_bias(d, patterns, slope):
    ad = jnp.abs(d)
    cnt = jnp.zeros(d.shape, F32)
    for window, dil in patterns:
        hit = ((d & (dil - 1)) == 0) & (ad <= window // 2)
        cnt = cnt + jnp.where(hit, 1.0, 0.0)
    bias = jnp.log(jnp.maximum(cnt, 1.0)) - slope * ad.astype(F32)
    return jnp.where(cnt > 0.0, bias, NEG)


def _dil_kernel(slopes_ref, *refs, seq):
    qs, ks, vs = refs[0:DIL_RES], refs[DIL_RES:2 * DIL_RES], refs[2 * DIL_RES:3 * DIL_RES]
    o_ref, tbn_ref, tbf_ref, out_ref = refs[3 * DIL_RES:]
    h = pl.program_id(0)
    sub = seq // DIL_RES
    n_tiles = sub // DIL_TQ
    far_off = sub - DIL_TQ
    near_pat, far_pat = DIL_PATTERNS[:1], DIL_PATTERNS[1:]

    def near_start(a0):
        return min(max(a0 - DIL_NEAR_PAD, 0), sub - DIL_NEAR)

    shifts = sorted({near_start(t * DIL_TQ) - t * DIL_TQ for t in range(n_tiles)})
    assert len(shifts) == tbn_ref.shape[1]

    @pl.when(pl.program_id(1) == 0)
    def _():
        slope = slopes_ref[h]
        qi = lax.broadcasted_iota(jnp.int32, (DIL_TQ, DIL_NEAR), 0)
        kk = lax.broadcasted_iota(jnp.int32, (DIL_TQ, DIL_NEAR), 1)
        for r in range(DIL_RES):
            for si, sh in enumerate(shifts):
                for r2 in range(DIL_RES):
                    d = DIL_RES * (kk + sh - qi) + (r2 - r)
                    tbn_ref[r, si, :, r2 * DIL_NEAR:(r2 + 1) * DIL_NEAR] = _dil_bias(d, near_pat, slope)
        qf = lax.broadcasted_iota(jnp.int32, (DIL_TQ, 128), 0)
        lf = lax.broadcasted_iota(jnp.int32, (DIL_TQ, 128), 1)
        for cb in range((sub + far_off) // 128):
            d = DIL_RES * (lf + (cb * 128 - far_off) - qf)
            tbf_ref[:, cb * 128:(cb + 1) * 128] = _dil_bias(d, far_pat, slope)

    tiles = [(r, t) for r in range(DIL_RES) for t in range(n_tiles)]
    for g0 in range(0, len(tiles), DIL_TILES_PER_TRIP):
        group = tiles[g0:g0 + DIL_TILES_PER_TRIP]
        ss, vcats = [], []
        for r, t in group:
            a0 = t * DIL_TQ
            ws = near_start(a0)
            si = shifts.index(ws - a0)
            kcat = jnp.concatenate([ks[r2][ws:ws + DIL_NEAR, :] for r2 in range(DIL_RES)] + [ks[r][...]], axis=0)
            vcats.append(jnp.concatenate([vs[r2][ws:ws + DIL_NEAR, :] for r2 in range(DIL_RES)] + [vs[r][...]], axis=0))
            bias = jnp.concatenate([tbn_ref[r, si], tbf_ref[:, far_off - a0:far_off - a0 + sub]], axis=1)
            ss.append(_dot_nt(qs[r][a0:a0 + DIL_TQ, :], kcat) * SCALE + bias)
        ps, dens = [], []
        for s in ss:
            m = jnp.max(s, axis=-1, keepdims=True)
            p = jnp.exp(s - m)
            dens.append(jnp.sum(p, axis=-1, keepdims=True))
            ps.append(p.astype(BF16))
        for (r, t), p, vcat, den in zip(group, ps, vcats, dens):
            out_ref[pl.ds(r + DIL_RES * t * DIL_TQ, DIL_TQ, stride=DIL_RES), :] = _dot(p, vcat) / den
    o_ref[...] = out_ref[...].astype(o_ref.dtype)


def dilated_attention(u, slopes):
    b, s, cols = u.shape
    sub = s // DIL_RES
    n_blk = cols // HEAD_DIM
    n_shifts = min(sub // DIL_TQ, 3)
    ur = u.reshape(b, sub, DIL_RES * cols)
    col = lambda c0, r: pl.BlockSpec((None, sub, HEAD_DIM), lambda h, n: (n, 0, r * n_blk + c0 + h))
    return pl.pallas_call(
        functools.partial(_dil_kernel, seq=s),
        grid=(N_HEADS, b),
        in_specs=[pl.BlockSpec(memory_space=pltpu.SMEM)]
                 + [col(c0, r) for c0 in (COL_DQ, COL_DK, COL_DV) for r in range(DIL_RES)],
        out_specs=pl.BlockSpec((None, s, HEAD_DIM), lambda h, n: (n, 0, h)),
        out_shape=jax.ShapeDtypeStruct((b, s, GROUP_WIDTH), BF16),
        scratch_shapes=[pltpu.VMEM((DIL_RES, n_shifts, DIL_TQ, DIL_RES * DIL_NEAR), F32),
                        pltpu.VMEM((DIL_TQ, 2 * sub - DIL_TQ), F32),
                        pltpu.VMEM((s, HEAD_DIM), F32)],
        compiler_params=_params("arbitrary", "arbitrary"),
        name="dilated_attn",
    )(slopes, *([ur] * (3 * DIL_RES)))


def kernel(x, ffn1_norm, ffn1_w_in, ffn1_w_out, mix_norm, w_in, na_rpb, conv_w, conv_b, conv_ln_g,
           conv_ln_b, swa_sink, branch_norm, w_out, ffn2_norm, ffn2_w_in, ffn2_w_out, final_norm):
    b, s, d = x.shape
    depth = w_in.shape[0]
    n_slopes = 2 * N_HEADS
    slopes = 2.0 ** (-8.0 * jnp.arange(1, n_slopes + 1, dtype=F32) / n_slopes)
    sl_c, sl_d = slopes[:N_HEADS], slopes[N_HEADS:]

    fold = lambda w, g: (w * g[:, :, None]).astype(BF16)
    ffn1_wi, ffn1_wo = fold(ffn1_w_in, ffn1_norm), ffn1_w_out.astype(BF16)
    ffn2_wi, ffn2_wo = fold(ffn2_w_in, ffn2_norm), ffn2_w_out.astype(BF16)
    mix_wi = fold(w_in, mix_norm)
    mix_wo = fold(w_out, branch_norm.reshape(depth, -1))

    h = x.reshape(b * s, d)
    hb, ss = prep(h)
    for l in range(depth):
        act = ffn_in(hb, ss, ffn1_wi, l)
        h, hb, ss = res_out(act, ffn1_wo, l, h, 0.5)

        ua, uc = mix_in(hb, ss, mix_wi, l)
        ua = ua.reshape(b, s, ATTN_COLS)
        uc = uc.reshape(b, s, 2 * GROUP_WIDTH)
        oa = neighbourhood_attention(ua, na_bias(na_rpb[l]))
        ob = conformer_conv(uc, conv_w[l], conv_b[l], conv_ln_g[l], conv_ln_b[l])
        oc = windowed_gqa(ua, sl_c, swa_sink[l])
        od = dilated_attention(ua, sl_d)
        outs = [o.reshape(b * s, GROUP_WIDTH) for o in (oa, ob, oc, od)]
        h, hb, ss = mix_out(outs, mix_wo, l, h)

        act = ffn_in(hb, ss, ffn2_wi, l)
        h, hb, ss = res_out(act, ffn2_wo, l, h, 0.5)
    return final_rms(h, final_norm).reshape(b, s, d)
```

```python
import functools

import jax
import jax.numpy as jnp
from jax import lax
from jax.experimental import pallas as pl
from jax.experimental.pallas import tpu as pltpu

D_MODEL = 4096
HEAD_DIM = 128
GROUP_WIDTH = 1024
N_HEADS = GROUP_WIDTH // HEAD_DIM
GRID_W = 64
NA_ROWS = 8
NA_COLS = 16
NA_ROWS_PER_TRIP = 16
CONV_WIDTH = 31
SUBLANES = 8
CONV_HALO = 16
SWA_KV_HEADS = 2
SWA_RADIUS = 128
SWA_BLOCKS_PER_TRIP = 2
DIL_PATTERNS = ((128, 1), (512, 4), (2048, 16))
EPS = 1e-6
NEG = -1e30
SCALE = HEAD_DIM ** -0.5

F32 = jnp.float32
BF16 = jnp.bfloat16

W_CONV_LO, W_CONV_HI = 24, 40
COL_AQ, COL_AK, COL_AV = 0, 8, 16
COL_CQ, COL_CK, COL_CV = 24, 32, 34
COL_DQ, COL_DK, COL_DV = 36, 44, 52
ATTN_COLS = 60 * HEAD_DIM

VMEM_LIMIT = 56 * 1024 * 1024


def _params(*sem):
    return pltpu.CompilerParams(dimension_semantics=sem, vmem_limit_bytes=VMEM_LIMIT)


def _dot(a, b):
    return jnp.dot(a, b, preferred_element_type=F32)


def _dot_nt(a, b):
    return lax.dot_general(a, b, (((1,), (1,)), ((), ())), preferred_element_type=F32)


def _rms(x, g):
    ms = jnp.mean(x * x, axis=-1, keepdims=True)
    return x * lax.rsqrt(ms + EPS) * g


STAT_LANES = 128


def _row_rsqrt(ss_ref, d):
    return lax.rsqrt(ss_ref[:, 0:1] / d + EPS)


def _emit_stream(y, j, o_ref, ob_ref, ss_ref):
    o_ref[...] = y
    ob_ref[...] = y.astype(BF16)
    part = jnp.broadcast_to(jnp.sum(y * y, axis=-1, keepdims=True), ss_ref.shape)

    @pl.when(j == 0)
    def _():
        ss_ref[...] = part

    @pl.when(j > 0)
    def _():
        ss_ref[...] += part


def _stream_out(t, n, tm, tn):
    specs = [pl.BlockSpec((tm, tn), lambda i, j: (i, j)),
             pl.BlockSpec((tm, tn), lambda i, j: (i, j)),
             pl.BlockSpec((tm, STAT_LANES), lambda i, j: (i, 0))]
    shapes = [jax.ShapeDtypeStruct((t, n), F32),
              jax.ShapeDtypeStruct((t, n), BF16),
              jax.ShapeDtypeStruct((t, STAT_LANES), F32)]
    return specs, shapes


def _prep_kernel(x_ref, ob_ref, ss_ref):
    x = x_ref[...]
    ob_ref[...] = x.astype(BF16)
    ss_ref[...] = jnp.broadcast_to(jnp.sum(x * x, axis=-1, keepdims=True), ss_ref.shape)


def prep(x, tm=512):
    t, d = x.shape
    return pl.pallas_call(
        _prep_kernel,
        grid=(t // tm,),
        in_specs=[pl.BlockSpec((tm, d), lambda i: (i, 0))],
        out_specs=[pl.BlockSpec((tm, d), lambda i: (i, 0)),
                   pl.BlockSpec((tm, STAT_LANES), lambda i: (i, 0))],
        out_shape=[jax.ShapeDtypeStruct((t, d), BF16),
                   jax.ShapeDtypeStruct((t, STAT_LANES), F32)],
        compiler_params=_params("parallel"),
        name="prep",
    )(x)


def _mix_in_kernel(hb_ref, ss_ref, w_ref, o_ref):
    y = _dot(hb_ref[...], w_ref[...]) * _row_rsqrt(ss_ref, hb_ref.shape[-1])
    o_ref[...] = y.astype(o_ref.dtype)


def _mix_in_call(hb, ss, w, layer, w_tile, n_out, out_dtype, tm, tn):
    t, d = hb.shape
    return pl.pallas_call(
        _mix_in_kernel,
        grid=(t // tm, n_out // tn),
        in_specs=[pl.BlockSpec((tm, d), lambda i, j: (i, 0)),
                  pl.BlockSpec((tm, STAT_LANES), lambda i, j: (i, 0)),
                  pl.BlockSpec((None, d, tn), lambda i, j: (layer, 0, w_tile(j)))],
        out_specs=pl.BlockSpec((tm, tn), lambda i, j: (i, j)),
        out_shape=jax.ShapeDtypeStruct((t, n_out), out_dtype),
        compiler_params=_params("parallel", "arbitrary"),
        name="mix_in",
    )(hb, ss, w)


def mix_in(hb, ss, w, layer, tm=1024, tn=512):
    n = w.shape[2]
    conv_lo = W_CONV_LO * HEAD_DIM // tn
    n_conv = (W_CONV_HI - W_CONV_LO) * HEAD_DIM // tn
    skip_conv = lambda j: jnp.where(j < conv_lo, j, j + n_conv)
    ua = _mix_in_call(hb, ss, w, layer, skip_conv, n - n_conv * tn, BF16, tm, tn)
    uc = _mix_in_call(hb, ss, w, layer, lambda j: j + conv_lo, n_conv * tn, F32, tm, tn)
    return ua, uc


def _ffn_in_kernel(hb_ref, ss_ref, wg_ref, wu_ref, o_ref):
    r = _row_rsqrt(ss_ref, hb_ref.shape[-1])
    x = hb_ref[...]
    gate = _dot(x, wg_ref[...]) * r
    up = _dot(x, wu_ref[...]) * r
    o_ref[...] = (gate * jax.nn.sigmoid(gate) * up).astype(o_ref.dtype)


def ffn_in(hb, ss, w, layer, tm=1024, tf=512):
    t, d = hb.shape
    f = w.shape[2] // 2
    nf = f // tf
    return pl.pallas_call(
        _ffn_in_kernel,
        grid=(t // tm, nf),
        in_specs=[pl.BlockSpec((tm, d), lambda i, j: (i, 0)),
                  pl.BlockSpec((tm, STAT_LANES), lambda i, j: (i, 0)),
                  pl.BlockSpec((None, d, tf), lambda i, j: (layer, 0, j)),
                  pl.BlockSpec((None, d, tf), lambda i, j: (layer, 0, j + nf))],
        out_specs=pl.BlockSpec((tm, tf), lambda i, j: (i, j)),
        out_shape=jax.ShapeDtypeStruct((t, f), BF16),
        compiler_params=_params("parallel", "arbitrary"),
        name="ffn_in",
    )(hb, ss, w, w)


def _res_out_kernel(a_ref, w_ref, h_ref, o_ref, ob_ref, ss_ref, *, scale):
    y = h_ref[...] + scale * _dot(a_ref[...], w_ref[...])
    _emit_stream(y, pl.program_id(1), o_ref, ob_ref, ss_ref)


def res_out(a, w, layer, h, scale, tm=1024, tn=512):
    t, k = a.shape
    n = w.shape[2]
    out_specs, out_shape = _stream_out(t, n, tm, tn)
    return pl.pallas_call(
        functools.partial(_res_out_kernel, scale=scale),
        grid=(t // tm, n // tn),
        in_specs=[pl.BlockSpec((tm, k), lambda i, j: (i, 0)),
                  pl.BlockSpec((None, k, tn), lambda i, j: (layer, 0, j)),
                  pl.BlockSpec((tm, tn), lambda i, j: (i, j))],
        out_specs=out_specs,
        out_shape=out_shape,
        compiler_params=_params("parallel", "arbitrary"),
        name="res_out",
    )(a, w, h)


def _mix_out_kernel(oa_ref, ob_ref, oc_ref, od_ref, w_ref, h_ref, o_ref, hb_ref, ss_ref, rs_ref):
    j = pl.program_id(1)
    groups = (oa_ref, ob_ref, oc_ref, od_ref)

    @pl.when(j == 0)
    def _():
        for gi, r in enumerate(groups):
            x = r[...].astype(F32)
            ms = jnp.mean(x * x, axis=-1, keepdims=True)
            rs_ref[gi] = jnp.broadcast_to(lax.rsqrt(ms + EPS), rs_ref.shape[1:])

    y = h_ref[...]
    for gi, r in enumerate(groups):
        lo = gi * GROUP_WIDTH
        y = y + rs_ref[gi][:, 0:1] * _dot(r[...], w_ref[lo:lo + GROUP_WIDTH, :])
    _emit_stream(y, j, o_ref, hb_ref, ss_ref)


def mix_out(outs, w, layer, h, tm=1024, tn=512):
    t, d = h.shape
    gw = GROUP_WIDTH
    grp = pl.BlockSpec((tm, gw), lambda i, j: (i, 0))
    out_specs, out_shape = _stream_out(t, d, tm, tn)
    return pl.pallas_call(
        _mix_out_kernel,
        grid=(t // tm, d // tn),
        in_specs=[grp, grp, grp, grp,
                  pl.BlockSpec((None, len(outs) * gw, tn), lambda i, j: (layer, 0, j)),
                  pl.BlockSpec((tm, tn), lambda i, j: (i, j))],
        out_specs=out_specs,
        out_shape=out_shape,
        scratch_shapes=[pltpu.VMEM((len(outs), tm, STAT_LANES), F32)],
        compiler_params=_params("parallel", "arbitrary"),
        name="mix_out",
    )(*outs, w, h)


def _final_norm_kernel(h_ref, g_ref, o_ref):
    o_ref[...] = _rms(h_ref[...], g_ref[...])


def final_rms(h, g, tm=512):
    t, d = h.shape
    return pl.pallas_call(
        _final_norm_kernel,
        grid=(t // tm,),
        in_specs=[pl.BlockSpec((tm, d), lambda i: (i, 0)),
                  pl.BlockSpec((1, d), lambda i: (0, 0))],
        out_specs=pl.BlockSpec((tm, d), lambda i: (i, 0)),
        out_shape=jax.ShapeDtypeStruct((t, d), F32),
        compiler_params=_params("parallel"),
        name="final_norm",
    )(h, g.reshape(1, d))


def _na_bias_kernel(rpb_ref, o_ref):
    h = pl.program_id(0)
    qc = lax.broadcasted_iota(jnp.int32, (GRID_W, GRID_W), 0)
    kc = lax.broadcasted_iota(jnp.int32, (GRID_W, GRID_W), 1)
    diff = kc - qc + (NA_COLS - 1)
    cstart = jnp.clip(qc - NA_COLS // 2, 0, GRID_W - NA_COLS)
    col_ok = (kc >= cstart) & (kc < cstart + NA_COLS)
    tiles = []
    for row in range(2 * NA_ROWS - 1):
        tile = jnp.zeros((GRID_W, GRID_W), F32)
        for d in range(2 * NA_COLS - 1):
            tile = jnp.where(diff == d, rpb_ref[h, row, d], tile)
        tiles.append(jnp.where(col_ok, tile, NEG))
    for oi in range(NA_ROWS):
        for kr in range(NA_ROWS):
            o_ref[oi, :, kr * GRID_W:(kr + 1) * GRID_W] = tiles[kr - oi + (NA_ROWS - 1)]


def na_bias(rpb):
    nh = rpb.shape[0]
    return pl.pallas_call(
        _na_bias_kernel,
        grid=(nh,),
        in_specs=[pl.BlockSpec(memory_space=pltpu.SMEM)],
        out_specs=pl.BlockSpec((None, NA_ROWS, GRID_W, NA_ROWS * GRID_W), lambda h: (h, 0, 0, 0)),
        out_shape=jax.ShapeDtypeStruct((nh, NA_ROWS, GRID_W, NA_ROWS * GRID_W), F32),
        compiler_params=_params("arbitrary"),
        name="na_bias",
    )(rpb)


def _na_kernel(q_ref, k_ref, v_ref, bias_ref, o_ref, *, rows):
    win = NA_ROWS * GRID_W

    def body(g, carry):
        ss, vws, q0s = [], [], []
        for t in range(NA_ROWS_PER_TRIP):
            r = g * NA_ROWS_PER_TRIP + t
            start_row = jnp.clip(r - NA_ROWS // 2, 0, rows - NA_ROWS)
            start = pl.multiple_of(start_row * GRID_W, GRID_W)
            q0 = pl.multiple_of(r * GRID_W, GRID_W)
            q = q_ref[pl.ds(q0, GRID_W), :]
            kw = k_ref[pl.ds(start, win), :]
            vws.append(v_ref[pl.ds(start, win), :])
            q0s.append(q0)
            ss.append(_dot_nt(q, kw) * SCALE + bias_ref[r - start_row])
        ps, dens = [], []
        for s in ss:
            m = jnp.max(s, axis=-1, keepdims=True)
            p = jnp.exp(s - m)
            dens.append(jnp.sum(p, axis=-1, keepdims=True))
            ps.append(p.astype(BF16))
        for q0, p, vw, den in zip(q0s, ps, vws, dens):
            o_ref[pl.ds(q0, GRID_W), :] = (_dot(p, vw) / den).astype(o_ref.dtype)
        return carry

    lax.fori_loop(0, rows // NA_ROWS_PER_TRIP, body, 0)


def neighbourhood_attention(u, bias):
    b, s, _ = u.shape
    rows = s // GRID_W
    col = lambda c0: pl.BlockSpec((None, s, HEAD_DIM), lambda h, n: (n, 0, c0 + h))
    return pl.pallas_call(
        functools.partial(_na_kernel, rows=rows),
        grid=(N_HEADS, b),
        in_specs=[col(COL_AQ), col(COL_AK), col(COL_AV),
                  pl.BlockSpec((None, NA_ROWS, GRID_W, NA_ROWS * GRID_W), lambda h, n: (h, 0, 0, 0))],
        out_specs=pl.BlockSpec((None, s, HEAD_DIM), lambda h, n: (n, 0, h)),
        out_shape=jax.ShapeDtypeStruct((b, s, GROUP_WIDTH), BF16),
        compiler_params=_params("arbitrary", "arbitrary"),
        name="na_attn",
    )(u, u, u, bias)


def _glu(a, g):
    return a * jax.nn.sigmoid(g)


def _conv_kernel(a_ref, g_ref, ap_ref, gp_ref, an_ref, gn_ref, w_ref, b_ref, lg_ref, lb_ref,
                 o_ref, hp_ref, acc_ref, hs_ref, *, ts, n_tiles):
    i = pl.program_id(1)
    c = a_ref.shape[-1]
    prev_ok = (i > 0).astype(F32)
    next_ok = (i < n_tiles - 1).astype(F32)
    hp_ref[0:CONV_HALO, :] = _glu(ap_ref[...], gp_ref[...]) * prev_ok
    hp_ref[CONV_HALO:CONV_HALO + ts, :] = _glu(a_ref[...], g_ref[...])
    hp_ref[CONV_HALO + ts:2 * CONV_HALO + ts, :] = _glu(an_ref[...], gn_ref[...]) * next_ok

    shift = CONV_HALO - CONV_WIDTH // 2
    slab = ts + 2 * CONV_HALO - SUBLANES
    for rem in range(1, SUBLANES):
        hs_ref[rem - 1] = hp_ref[rem:rem + slab, :]
    for cc in range(c // 128):
        lanes = slice(cc * 128, (cc + 1) * 128)
        acc = jnp.zeros((ts, 128), F32)
        for rem in range(SUBLANES):
            for base in range(0, slab - ts + 1, SUBLANES):
                k = base + rem - shift
                if 0 <= k < CONV_WIDTH:
                    if rem == 0:
                        rows = hp_ref[base:base + ts, lanes]
                    else:
                        rows = hs_ref[rem - 1, base:base + ts, lanes]
                    acc = acc + rows * w_ref[k:k + 1, lanes]
        acc_ref[:, lanes] = acc + b_ref[:, lanes]

    hf = acc_ref[...]
    mu = jnp.mean(hf, axis=-1, keepdims=True)
    xc = hf - mu
    var = jnp.mean(xc * xc, axis=-1, keepdims=True)
    y = xc * lax.rsqrt(var + EPS) * lg_ref[...] + lb_ref[...]
    o_ref[...] = (y * jax.nn.sigmoid(y)).astype(o_ref.dtype)


def conformer_conv(uc, w, bias, ln_g, ln_b, ts=256):
    b, s, c2 = uc.shape
    c = c2 // 2
    n_tiles = s // ts
    hb = ts // CONV_HALO
    last = s // CONV_HALO - 1
    cur = lambda half: pl.BlockSpec((None, ts, c), lambda n, i: (n, i, half))
    prev = lambda half: pl.BlockSpec((None, CONV_HALO, c),
                                     lambda n, i: (n, jnp.maximum(i * hb - 1, 0), half))
    nxt = lambda half: pl.BlockSpec((None, CONV_HALO, c),
                                    lambda n, i: (n, jnp.minimum((i + 1) * hb, last), half))
    vec = pl.BlockSpec((1, c), lambda n, i: (0, 0))
    return pl.pallas_call(
        functools.partial(_conv_kernel, ts=ts, n_tiles=n_tiles),
        grid=(b, n_tiles),
        in_specs=[cur(0), cur(1), prev(0), prev(1), nxt(0), nxt(1),
                  pl.BlockSpec((CONV_WIDTH, c), lambda n, i: (0, 0)), vec, vec, vec],
        out_specs=pl.BlockSpec((None, ts, c), lambda n, i: (n, i, 0)),
        out_shape=jax.ShapeDtypeStruct((b, s, c), BF16),
        scratch_shapes=[pltpu.VMEM((ts + 2 * CONV_HALO, c), F32), pltpu.VMEM((ts, c), F32),
                        pltpu.VMEM((SUBLANES - 1, ts + 2 * CONV_HALO - SUBLANES, c), F32)],
        compiler_params=_params("parallel", "arbitrary"),
        name="conformer_conv",
    )(uc, uc, uc, uc, uc, uc, w, bias.reshape(1, c), ln_g.reshape(1, c), ln_b.reshape(1, c))


def _swa_kernel(slopes_ref, sink_ref, q_ref, k_ref, v_ref, o_ref, *, seq):
    kvh = pl.program_id(1)
    blk = SWA_RADIUS
    win = 3 * blk
    group = N_HEADS // SWA_KV_HEADS
    qi = lax.broadcasted_iota(jnp.int32, (blk, win), 0)
    ki = lax.broadcasted_iota(jnp.int32, (blk, win), 1)

    def body(it, carry):
        chains = []
        for t in range(SWA_BLOCKS_PER_TRIP):
            q0 = pl.multiple_of((it * SWA_BLOCKS_PER_TRIP + t) * blk, blk)
            start = pl.multiple_of(jnp.clip(q0 - blk, 0, seq - win), blk)
            kw = k_ref[pl.ds(start, win), :]
            vw = v_ref[pl.ds(start, win), :]
            q4 = q_ref[pl.ds(q0, blk), :]
            qs = jnp.concatenate([q4[:, g * HEAD_DIM:(g + 1) * HEAD_DIM] for g in range(group)], axis=0)
            chains.append((q0, start, vw, _dot_nt(qs, kw) * SCALE))
        outs = []
        for q0, start, vw, s_all in chains:
            dist = jnp.abs(ki - qi + (start - q0))
            valid = dist <= SWA_RADIUS
            distf = dist.astype(F32)
            ps, gates = [], []
            for g in range(group):
                head = kvh * group + g
                s = s_all[g * blk:(g + 1) * blk]
                s = jnp.where(valid, s - slopes_ref[head] * distf, NEG)
                m = jnp.max(s, axis=-1, keepdims=True)
                p = jnp.exp(s - m)
                den = jnp.sum(p, axis=-1, keepdims=True)
                lse = m + jnp.log(den)
                gates.append(jax.nn.sigmoid(lse - sink_ref[head]) / den)
                ps.append(p.astype(BF16))
            outs.append((q0, vw, jnp.concatenate(ps, axis=0), gates))
        for q0, vw, p_all, gates in outs:
            o_all = _dot(p_all, vw)
            for g in range(group):
                o = o_all[g * blk:(g + 1) * blk] * gates[g]
                o_ref[pl.ds(q0, blk), g * HEAD_DIM:(g + 1) * HEAD_DIM] = o.astype(o_ref.dtype)
        return carry

    lax.fori_loop(0, seq // (blk * SWA_BLOCKS_PER_TRIP), body, 0)


def windowed_gqa(u, slopes, sink):
    b, s, _ = u.shape
    group = N_HEADS // SWA_KV_HEADS
    qw = group * HEAD_DIM
    smem = pl.BlockSpec(memory_space=pltpu.SMEM)
    return pl.pallas_call(
        functools.partial(_swa_kernel, seq=s),
        grid=(b, SWA_KV_HEADS),
        in_specs=[smem, smem,
                  pl.BlockSpec((None, s, qw), lambda n, kv: (n, 0, COL_CQ // group + kv)),
                  pl.BlockSpec((None, s, HEAD_DIM), lambda n, kv: (n, 0, COL_CK + kv)),
                  pl.BlockSpec((None, s, HEAD_DIM), lambda n, kv: (n, 0, COL_CV + kv))],
        out_specs=pl.BlockSpec((None, s, qw), lambda n, kv: (n, 0, kv)),
        out_shape=jax.ShapeDtypeStruct((b, s, GROUP_WIDTH), BF16),
        compiler_params=_params("parallel", "arbitrary"),
        name="windowed_gqa",
    )(slopes, sink, u, u, u)


DIL_TQ = 128
DIL_RES = 4
DIL_NEAR = 160
DIL_NEAR_PAD = 16
DIL_TILES_PER_TRIP = 2


def _dil_bias(d, patterns, slope):
    ad = jnp.abs(d)
    cnt = jnp.zeros(d.shape, F32)
    for window, dil in patterns:
        hit = ((d & (dil - 1)) == 0) & (ad <= window // 2)
        cnt = cnt + jnp.where(hit, 1.0, 0.0)
    bias = jnp.log(jnp.maximum(cnt, 1.0)) - slope * ad.astype(F32)
    return jnp.where(cnt > 0.0, bias, NEG)


def _dil_kernel(slopes_ref, q_ref, k_ref, v_ref, o_ref, tbn_ref, tbf_ref, out_ref, stage_ref, cls_ref, *, seq):
    h = pl.program_id(0)
    sub = seq // DIL_RES

    for xi, x_ref in enumerate((q_ref, k_ref, v_ref)):
        stage_ref[xi] = x_ref[...].astype(F32)
        for r in range(DIL_RES):
            cls_ref[xi, r] = stage_ref[xi, pl.ds(r, sub, stride=DIL_RES), :].astype(BF16)
    qs, ks, vs = ([cls_ref.at[xi, r] for r in range(DIL_RES)] for xi in range(3))
    n_tiles = sub // DIL_TQ
    far_off = sub - DIL_TQ
    near_pat, far_pat = DIL_PATTERNS[:1], DIL_PATTERNS[1:]

    def near_start(a0):
        return min(max(a0 - DIL_NEAR_PAD, 0), sub - DIL_NEAR)

    shifts = sorted({near_start(t * DIL_TQ) - t * DIL_TQ for t in range(n_tiles)})
    assert len(shifts) == tbn_ref.shape[1]

    @pl.when(pl.program_id(1) == 0)
    def _():
        slope = slopes_ref[h]
        qi = lax.broadcasted_iota(jnp.int32, (DIL_TQ, DIL_NEAR), 0)
        kk = lax.broadcasted_iota(jnp.int32, (DIL_TQ, DIL_NEAR), 1)
        for r in range(DIL_RES):
            for si, sh in enumerate(shifts):
                for r2 in range(DIL_RES):
                    d = DIL_RES * (kk + sh - qi) + (r2 - r)
                    tbn_ref[r, si, :, r2 * DIL_NEAR:(r2 + 1) * DIL_NEAR] = _dil_bias(d, near_pat, slope)
        qf = lax.broadcasted_iota(jnp.int32, (DIL_TQ, 128), 0)
        lf = lax.broadcasted_iota(jnp.int32, (DIL_TQ, 128), 1)
        for cb in range((sub + far_off) // 128):
            d = DIL_RES * (lf + (cb * 128 - far_off) - qf)
            tbf_ref[:, cb * 128:(cb + 1) * 128] = _dil_bias(d, far_pat, slope)

    tiles = [(r, t) for r in range(DIL_RES) for t in range(n_tiles)]
    for g0 in range(0, len(tiles), DIL_TILES_PER_TRIP):
        group = tiles[g0:g0 + DIL_TILES_PER_TRIP]
        ss, vcats = [], []
        for r, t in group:
            a0 = t * DIL_TQ
            ws = near_start(a0)
            si = shifts.index(ws - a0)
            kcat = jnp.concatenate([ks[r2][ws:ws + DIL_NEAR, :] for r2 in range(DIL_RES)] + [ks[r][...]], axis=0)
            vcats.append(jnp.concatenate([vs[r2][ws:ws + DIL_NEAR, :] for r2 in range(DIL_RES)] + [vs[r][...]], axis=0))
            bias = jnp.concatenate([tbn_ref[r, si], tbf_ref[:, far_off - a0:far_off - a0 + sub]], axis=1)
            ss.append(_dot_nt(qs[r][a0:a0 + DIL_TQ, :], kcat) * SCALE + bias)
        ps, dens = [], []
        for s in ss:
            m = jnp.max(s, axis=-1, keepdims=True)
            p = jnp.exp(s - m)
            dens.append(jnp.sum(p, axis=-1, keepdims=True))
            ps.append(p.astype(BF16))
        for (r, t), p, vcat, den in zip(group, ps, vcats, dens):
            out_ref[pl.ds(r + DIL_RES * t * DIL_TQ, DIL_TQ, stride=DIL_RES), :] = _dot(p, vcat) / den
    o_ref[...] = out_ref[...].astype(o_ref.dtype)


def dilated_attention(u, slopes):
    b, s, _ = u.shape
    sub = s // DIL_RES
    n_shifts = min(sub // DIL_TQ, 3)
    col = lambda c0: pl.BlockSpec((None, s, HEAD_DIM), lambda h, n: (n, 0, c0 + h))
    return pl.pallas_call(
        functools.partial(_dil_kernel, seq=s),
        grid=(N_HEADS, b),
        in_specs=[pl.BlockSpec(memory_space=pltpu.SMEM),
                  col(COL_DQ), col(COL_DK), col(COL_DV)],
        out_specs=pl.BlockSpec((None, s, HEAD_DIM), lambda h, n: (n, 0, h)),
        out_shape=jax.ShapeDtypeStruct((b, s, GROUP_WIDTH), BF16),
        scratch_shapes=[pltpu.VMEM((DIL_RES, n_shifts, DIL_TQ, DIL_RES * DIL_NEAR), F32),
                        pltpu.VMEM((DIL_TQ, 2 * sub - DIL_TQ), F32),
                        pltpu.VMEM((s, HEAD_DIM), F32),
                        pltpu.VMEM((3, s, HEAD_DIM), F32),
                        pltpu.VMEM((3, DIL_RES, sub, HEAD_DIM), BF16)],
        compiler_params=_params("arbitrary", "arbitrary"),
        name="dilated_attn",
    )(slopes, u, u, u)


def kernel(x, ffn1_norm, ffn1_w_in, ffn1_w_out, mix_norm, w_in, na_rpb, conv_w, conv_b, conv_ln_g,
           conv_ln_b, swa_sink, branch_norm, w_out, ffn2_norm, ffn2_w_in, ffn2_w_out, final_norm):
    b, s, d = x.shape
    depth = w_in.shape[0]
    n_slopes = 2 * N_HEADS
    slopes = 2.0 ** (-8.0 * jnp.arange(1, n_slopes + 1, dtype=F32) / n_slopes)
    sl_c, sl_d = slopes[:N_HEADS], slopes[N_HEADS:]

    fold = lambda w, g: (w * g[:, :, None]).astype(BF16)
    ffn1_wi, ffn1_wo = fold(ffn1_w_in, ffn1_norm), ffn1_w_out.astype(BF16)
    ffn2_wi, ffn2_wo = fold(ffn2_w_in, ffn2_norm), ffn2_w_out.astype(BF16)
    mix_wi = fold(w_in, mix_norm)
    mix_wo = fold(w_out, branch_norm.reshape(depth, -1))

    h = x.reshape(b * s, d)
    hb, ss = prep(h)
    for l in range(depth):
        act = ffn_in(hb, ss, ffn1_wi, l)
        h, hb, ss = res_out(act, ffn1_wo, l, h, 0.5)

        ua, uc = mix_in(hb, ss, mix_wi, l)
        ua = ua.reshape(b, s, ATTN_COLS)
        uc = uc.reshape(b, s, 2 * GROUP_WIDTH)
        oa = neighbourhood_attention(ua, na_bias(na_rpb[l]))
        ob = conformer_conv(uc, conv_w[l], conv_b[l], conv_ln_g[l], conv_ln_b[l])
        oc = windowed_gqa(ua, sl_c, swa_sink[l])
        od = dilated_attention(ua, sl_d)
        outs = [o.reshape(b * s, GROUP_WIDTH) for o in (oa, ob, oc, od)]
        h, hb, ss = mix_out(outs, mix_wo, l, h)

        act = ffn_in(hb, ss, ffn2_wi, l)
        h, hb, ss = res_out(act, ffn2_wo, l, h, 0.5)
    return final_rms(h, final_norm).reshape(b, s, d)
```

```python
import functools

import jax
import jax.numpy as jnp
from jax import lax
from jax.experimental import pallas as pl
from jax.experimental.pallas import tpu as pltpu

D_MODEL = 4096
HEAD_DIM = 128
GROUP_WIDTH = 1024
N_HEADS = GROUP_WIDTH // HEAD_DIM
GRID_W = 64
NA_ROWS = 8
NA_COLS = 16
NA_ROWS_PER_TRIP = 16
CONV_WIDTH = 31
SUBLANES = 8
CONV_HALO = 16
SWA_KV_HEADS = 2
SWA_RADIUS = 128
SWA_BLOCKS_PER_TRIP = 2
DIL_PATTERNS = ((128, 1), (512, 4), (2048, 16))
EPS = 1e-6
NEG = -1e30
SCALE = HEAD_DIM ** -0.5

F32 = jnp.float32
BF16 = jnp.bfloat16

W_CONV_LO, W_CONV_HI = 24, 40
COL_AQ, COL_AK, COL_AV = 0, 8, 16
COL_CQ, COL_CK, COL_CV = 24, 32, 34
COL_DQ, COL_DK, COL_DV = 36, 44, 52
ATTN_COLS = 60 * HEAD_DIM

VMEM_LIMIT = 56 * 1024 * 1024


def _params(*sem):
    return pltpu.CompilerParams(dimension_semantics=sem, vmem_limit_bytes=VMEM_LIMIT)


def _dot(a, b):
    return jnp.dot(a, b, preferred_element_type=F32)


def _dot_nt(a, b):
    return lax.dot_general(a, b, (((1,), (1,)), ((), ())), preferred_element_type=F32)


def _rms(x, g):
    ms = jnp.mean(x * x, axis=-1, keepdims=True)
    return x * lax.rsqrt(ms + EPS) * g


STAT_LANES = 128


def _row_rsqrt(ss_ref, d):
    return lax.rsqrt(ss_ref[:, 0:1] / d + EPS)


def _emit_stream(y, j, o_ref, ob_ref, ss_ref):
    o_ref[...] = y
    ob_ref[...] = y.astype(BF16)
    part = jnp.broadcast_to(jnp.sum(y * y, axis=-1, keepdims=True), ss_ref.shape)

    @pl.when(j == 0)
    def _():
        ss_ref[...] = part

    @pl.when(j > 0)
    def _():
        ss_ref[...] += part


def _stream_out(t, n, tm, tn):
    specs = [pl.BlockSpec((tm, tn), lambda i, j: (i, j)),
             pl.BlockSpec((tm, tn), lambda i, j: (i, j)),
             pl.BlockSpec((tm, STAT_LANES), lambda i, j: (i, 0))]
    shapes = [jax.ShapeDtypeStruct((t, n), F32),
              jax.ShapeDtypeStruct((t, n), BF16),
              jax.ShapeDtypeStruct((t, STAT_LANES), F32)]
    return specs, shapes


def _prep_kernel(x_ref, ob_ref, ss_ref):
    x = x_ref[...]
    ob_ref[...] = x.astype(BF16)
    ss_ref[...] = jnp.broadcast_to(jnp.sum(x * x, axis=-1, keepdims=True), ss_ref.shape)


def prep(x, tm=512):
    t, d = x.shape
    return pl.pallas_call(
        _prep_kernel,
        grid=(t // tm,),
        in_specs=[pl.BlockSpec((tm, d), lambda i: (i, 0))],
        out_specs=[pl.BlockSpec((tm, d), lambda i: (i, 0)),
                   pl.BlockSpec((tm, STAT_LANES), lambda i: (i, 0))],
        out_shape=[jax.ShapeDtypeStruct((t, d), BF16),
                   jax.ShapeDtypeStruct((t, STAT_LANES), F32)],
        compiler_params=_params("parallel"),
        name="prep",
    )(x)


def _mix_in_kernel(hb_ref, ss_ref, w_ref, o_ref):
    y = _dot(hb_ref[...], w_ref[...]) * _row_rsqrt(ss_ref, hb_ref.shape[-1])
    o_ref[...] = y.astype(o_ref.dtype)


def _mix_in_call(hb, ss, w, layer, w_tile, n_out, out_dtype, tm, tn):
    t, d = hb.shape
    return pl.pallas_call(
        _mix_in_kernel,
        grid=(t // tm, n_out // tn),
        in_specs=[pl.BlockSpec((tm, d), lambda i, j: (i, 0)),
                  pl.BlockSpec((tm, STAT_LANES), lambda i, j: (i, 0)),
                  pl.BlockSpec((None, d, tn), lambda i, j: (layer, 0, w_tile(j)))],
        out_specs=pl.BlockSpec((tm, tn), lambda i, j: (i, j)),
        out_shape=jax.ShapeDtypeStruct((t, n_out), out_dtype),
        compiler_params=_params("parallel", "arbitrary"),
        name="mix_in",
    )(hb, ss, w)


def mix_in(hb, ss, w, layer, tm=2048, tn=512):
    n = w.shape[2]
    conv_lo = W_CONV_LO * HEAD_DIM // tn
    n_conv = (W_CONV_HI - W_CONV_LO) * HEAD_DIM // tn
    skip_conv = lambda j: jnp.where(j < conv_lo, j, j + n_conv)
    ua = _mix_in_call(hb, ss, w, layer, skip_conv, n - n_conv * tn, BF16, tm, tn)
    uc = _mix_in_call(hb, ss, w, layer, lambda j: j + conv_lo, n_conv * tn, F32, tm, tn)
    return ua, uc


def _ffn_in_kernel(hb_ref, ss_ref, wg_ref, wu_ref, o_ref):
    r = _row_rsqrt(ss_ref, hb_ref.shape[-1])
    x = hb_ref[...]
    gate = _dot(x, wg_ref[...]) * r
    up = _dot(x, wu_ref[...]) * r
    o_ref[...] = (gate * jax.nn.sigmoid(gate) * up).astype(o_ref.dtype)


def ffn_in(hb, ss, w, layer, tm=1024, tf=512):
    t, d = hb.shape
    f = w.shape[2] // 2
    nf = f // tf
    return pl.pallas_call(
        _ffn_in_kernel,
        grid=(t // tm, nf),
        in_specs=[pl.BlockSpec((tm, d), lambda i, j: (i, 0)),
                  pl.BlockSpec((tm, STAT_LANES), lambda i, j: (i, 0)),
                  pl.BlockSpec((None, d, tf), lambda i, j: (layer, 0, j)),
                  pl.BlockSpec((None, d, tf), lambda i, j: (layer, 0, j + nf))],
        out_specs=pl.BlockSpec((tm, tf), lambda i, j: (i, j)),
        out_shape=jax.ShapeDtypeStruct((t, f), BF16),
        compiler_params=_params("parallel", "arbitrary"),
        name="ffn_in",
    )(hb, ss, w, w)


def _res_out_kernel(a_ref, w_ref, h_ref, o_ref, ob_ref, ss_ref, *, scale):
    y = h_ref[...] + scale * _dot(a_ref[...], w_ref[...])
    _emit_stream(y, pl.program_id(1), o_ref, ob_ref, ss_ref)


def res_out(a, w, layer, h, scale, tm=1024, tn=512):
    t, k = a.shape
    n = w.shape[2]
    out_specs, out_shape = _stream_out(t, n, tm, tn)
    return pl.pallas_call(
        functools.partial(_res_out_kernel, scale=scale),
        grid=(t // tm, n // tn),
        in_specs=[pl.BlockSpec((tm, k), lambda i, j: (i, 0)),
                  pl.BlockSpec((None, k, tn), lambda i, j: (layer, 0, j)),
                  pl.BlockSpec((tm, tn), lambda i, j: (i, j))],
        out_specs=out_specs,
        out_shape=out_shape,
        compiler_params=_params("parallel", "arbitrary"),
        name="res_out",
    )(a, w, h)


def _mix_out_kernel(oa_ref, ob_ref, oc_ref, od_ref, w_ref, h_ref, o_ref, hb_ref, ss_ref, rs_ref):
    j = pl.program_id(1)
    groups = (oa_ref, ob_ref, oc_ref, od_ref)

    @pl.when(j == 0)
    def _():
        for gi, r in enumerate(groups):
            x = r[...].astype(F32)
            ms = jnp.mean(x * x, axis=-1, keepdims=True)
            rs_ref[gi] = jnp.broadcast_to(lax.rsqrt(ms + EPS), rs_ref.shape[1:])

    y = h_ref[...]
    for gi, r in enumerate(groups):
        lo = gi * GROUP_WIDTH
        y = y + rs_ref[gi][:, 0:1] * _dot(r[...], w_ref[lo:lo + GROUP_WIDTH, :])
    _emit_stream(y, j, o_ref, hb_ref, ss_ref)


def mix_out(outs, w, layer, h, tm=1024, tn=512):
    t, d = h.shape
    gw = GROUP_WIDTH
    grp = pl.BlockSpec((tm, gw), lambda i, j: (i, 0))
    out_specs, out_shape = _stream_out(t, d, tm, tn)
    return pl.pallas_call(
        _mix_out_kernel,
        grid=(t // tm, d // tn),
        in_specs=[grp, grp, grp, grp,
                  pl.BlockSpec((None, len(outs) * gw, tn), lambda i, j: (layer, 0, j)),
                  pl.BlockSpec((tm, tn), lambda i, j: (i, j))],
        out_specs=out_specs,
        out_shape=out_shape,
        scratch_shapes=[pltpu.VMEM((len(outs), tm, STAT_LANES), F32)],
        compiler_params=_params("parallel", "arbitrary"),
        name="mix_out",
    )(*outs, w, h)


def _final_norm_kernel(h_ref, g_ref, o_ref):
    o_ref[...] = _rms(h_ref[...], g_ref[...])


def final_rms(h, g, tm=512):
    t, d = h.shape
    return pl.pallas_call(
        _final_norm_kernel,
        grid=(t // tm,),
        in_specs=[pl.BlockSpec((tm, d), lambda i: (i, 0)),
                  pl.BlockSpec((1, d), lambda i: (0, 0))],
        out_specs=pl.BlockSpec((tm, d), lambda i: (i, 0)),
        out_shape=jax.ShapeDtypeStruct((t, d), F32),
        compiler_params=_params("parallel"),
        name="final_norm",
    )(h, g.reshape(1, d))


def _na_bias_kernel(rpb_ref, o_ref):
    h = pl.program_id(0)
    qc = lax.broadcasted_iota(jnp.int32, (GRID_W, GRID_W), 0)
    kc = lax.broadcasted_iota(jnp.int32, (GRID_W, GRID_W), 1)
    diff = kc - qc + (NA_COLS - 1)
    cstart = jnp.clip(qc - NA_COLS // 2, 0, GRID_W - NA_COLS)
    col_ok = (kc >= cstart) & (kc < cstart + NA_COLS)
    tiles = []
    for row in range(2 * NA_ROWS - 1):
        tile = jnp.zeros((GRID_W, GRID_W), F32)
        for d in range(2 * NA_COLS - 1):
            tile = jnp.where(diff == d, rpb_ref[h, row, d], tile)
        tiles.append(jnp.where(col_ok, tile, NEG))
    for oi in range(NA_ROWS):
        for kr in range(NA_ROWS):
            o_ref[oi, :, kr * GRID_W:(kr + 1) * GRID_W] = tiles[kr - oi + (NA_ROWS - 1)]


def na_bias(rpb):
    nh = rpb.shape[0]
    return pl.pallas_call(
        _na_bias_kernel,
        grid=(nh,),
        in_specs=[pl.BlockSpec(memory_space=pltpu.SMEM)],
        out_specs=pl.BlockSpec((None, NA_ROWS, GRID_W, NA_ROWS * GRID_W), lambda h: (h, 0, 0, 0)),
        out_shape=jax.ShapeDtypeStruct((nh, NA_ROWS, GRID_W, NA_ROWS * GRID_W), F32),
        compiler_params=_params("arbitrary"),
        name="na_bias",
    )(rpb)


def _na_kernel(q_ref, k_ref, v_ref, bias_ref, o_ref, *, rows):
    win = NA_ROWS * GRID_W

    def body(g, carry):
        ss, vws, q0s = [], [], []
        for t in range(NA_ROWS_PER_TRIP):
            r = g * NA_ROWS_PER_TRIP + t
            start_row = jnp.clip(r - NA_ROWS // 2, 0, rows - NA_ROWS)
            start = pl.multiple_of(start_row * GRID_W, GRID_W)
            q0 = pl.multiple_of(r * GRID_W, GRID_W)
            q = q_ref[pl.ds(q0, GRID_W), :]
            kw = k_ref[pl.ds(start, win), :]
            vws.append(v_ref[pl.ds(start, win), :])
            q0s.append(q0)
            ss.append(_dot_nt(q, kw) * SCALE + bias_ref[r - start_row])
        ps, dens = [], []
        for s in ss:
            m = jnp.max(s, axis=-1, keepdims=True)
            p = jnp.exp(s - m)
            dens.append(jnp.sum(p, axis=-1, keepdims=True))
            ps.append(p.astype(BF16))
        for q0, p, vw, den in zip(q0s, ps, vws, dens):
            o_ref[pl.ds(q0, GRID_W), :] = (_dot(p, vw) / den).astype(o_ref.dtype)
        return carry

    lax.fori_loop(0, rows // NA_ROWS_PER_TRIP, body, 0)


def neighbourhood_attention(u, bias):
    b, s, _ = u.shape
    rows = s // GRID_W
    col = lambda c0: pl.BlockSpec((None, s, HEAD_DIM), lambda h, n: (n, 0, c0 + h))
    return pl.pallas_call(
        functools.partial(_na_kernel, rows=rows),
        grid=(N_HEADS, b),
        in_specs=[col(COL_AQ), col(COL_AK), col(COL_AV),
                  pl.BlockSpec((None, NA_ROWS, GRID_W, NA_ROWS * GRID_W), lambda h, n: (h, 0, 0, 0))],
        out_specs=pl.BlockSpec((None, s, HEAD_DIM), lambda h, n: (n, 0, h)),
        out_shape=jax.ShapeDtypeStruct((b, s, GROUP_WIDTH), BF16),
        compiler_params=_params("arbitrary", "arbitrary"),
        name="na_attn",
    )(u, u, u, bias)


def _glu(a, g):
    return a * jax.nn.sigmoid(g)


def _conv_kernel(a_ref, g_ref, ap_ref, gp_ref, an_ref, gn_ref, w_ref, b_ref, lg_ref, lb_ref,
                 o_ref, hp_ref, acc_ref, hs_ref, *, ts, n_tiles):
    i = pl.program_id(1)
    c = a_ref.shape[-1]
    prev_ok = (i > 0).astype(F32)
    next_ok = (i < n_tiles - 1).astype(F32)
    hp_ref[0:CONV_HALO, :] = _glu(ap_ref[...], gp_ref[...]) * prev_ok
    hp_ref[CONV_HALO:CONV_HALO + ts, :] = _glu(a_ref[...], g_ref[...])
    hp_ref[CONV_HALO + ts:2 * CONV_HALO + ts, :] = _glu(an_ref[...], gn_ref[...]) * next_ok

    shift = CONV_HALO - CONV_WIDTH // 2
    slab = ts + 2 * CONV_HALO - SUBLANES
    for rem in range(1, SUBLANES):
        hs_ref[rem - 1] = hp_ref[rem:rem + slab, :]
    for cc in range(c // 128):
        lanes = slice(cc * 128, (cc + 1) * 128)
        acc = jnp.zeros((ts, 128), F32)
        for rem in range(SUBLANES):
            for base in range(0, slab - ts + 1, SUBLANES):
                k = base + rem - shift
                if 0 <= k < CONV_WIDTH:
                    if rem == 0:
                        rows = hp_ref[base:base + ts, lanes]
                    else:
                        rows = hs_ref[rem - 1, base:base + ts, lanes]
                    acc = acc + rows * w_ref[k:k + 1, lanes]
        acc_ref[:, lanes] = acc + b_ref[:, lanes]

    hf = acc_ref[...]
    mu = jnp.mean(hf, axis=-1, keepdims=True)
    xc = hf - mu
    var = jnp.mean(xc * xc, axis=-1, keepdims=True)
    y = xc * lax.rsqrt(var + EPS) * lg_ref[...] + lb_ref[...]
    o_ref[...] = (y * jax.nn.sigmoid(y)).astype(o_ref.dtype)


def conformer_conv(uc, w, bias, ln_g, ln_b, ts=256):
    b, s, c2 = uc.shape
    c = c2 // 2
    n_tiles = s // ts
    hb = ts // CONV_HALO
    last = s // CONV_HALO - 1
    cur = lambda half: pl.BlockSpec((None, ts, c), lambda n, i: (n, i, half))
    prev = lambda half: pl.BlockSpec((None, CONV_HALO, c),
                                     lambda n, i: (n, jnp.maximum(i * hb - 1, 0), half))
    nxt = lambda half: pl.BlockSpec((None, CONV_HALO, c),
                                    lambda n, i: (n, jnp.minimum((i + 1) * hb, last), half))
    vec = pl.BlockSpec((1, c), lambda n, i: (0, 0))
    return pl.pallas_call(
        functools.partial(_conv_kernel, ts=ts, n_tiles=n_tiles),
        grid=(b, n_tiles),
        in_specs=[cur(0), cur(1), prev(0), prev(1), nxt(0), nxt(1),
                  pl.BlockSpec((CONV_WIDTH, c), lambda n, i: (0, 0)), vec, vec, vec],
        out_specs=pl.BlockSpec((None, ts, c), lambda n, i: (n, i, 0)),
        out_shape=jax.ShapeDtypeStruct((b, s, c), BF16),
        scratch_shapes=[pltpu.VMEM((ts + 2 * CONV_HALO, c), F32), pltpu.VMEM((ts, c), F32),
                        pltpu.VMEM((SUBLANES - 1, ts + 2 * CONV_HALO - SUBLANES, c), F32)],
        compiler_params=_params("parallel", "arbitrary"),
        name="conformer_conv",
    )(uc, uc, uc, uc, uc, uc, w, bias.reshape(1, c), ln_g.reshape(1, c), ln_b.reshape(1, c))


def _swa_kernel(slopes_ref, sink_ref, q_ref, k_ref, v_ref, o_ref, *, seq):
    kvh = pl.program_id(1)
    blk = SWA_RADIUS
    win = 3 * blk
    group = N_HEADS // SWA_KV_HEADS
    qi = lax.broadcasted_iota(jnp.int32, (blk, win), 0)
    ki = lax.broadcasted_iota(jnp.int32, (blk, win), 1)

    def body(it, carry):
        chains = []
        for t in range(SWA_BLOCKS_PER_TRIP):
            q0 = pl.multiple_of((it * SWA_BLOCKS_PER_TRIP + t) * blk, blk)
            start = pl.multiple_of(jnp.clip(q0 - blk, 0, seq - win), blk)
            kw = k_ref[pl.ds(start, win), :]
            vw = v_ref[pl.ds(start, win), :]
            q4 = q_ref[pl.ds(q0, blk), :]
            qs = jnp.concatenate([q4[:, g * HEAD_DIM:(g + 1) * HEAD_DIM] for g in range(group)], axis=0)
            chains.append((q0, start, vw, _dot_nt(qs, kw) * SCALE))
        outs = []
        for q0, start, vw, s_all in chains:
            dist = jnp.abs(ki - qi + (start - q0))
            valid = dist <= SWA_RADIUS
            distf = dist.astype(F32)
            ps, gates = [], []
            for g in range(group):
                head = kvh * group + g
                s = s_all[g * blk:(g + 1) * blk]
                s = jnp.where(valid, s - slopes_ref[head] * distf, NEG)
                m = jnp.max(s, axis=-1, keepdims=True)
                p = jnp.exp(s - m)
                den = jnp.sum(p, axis=-1, keepdims=True)
                lse = m + jnp.log(den)
                gates.append(jax.nn.sigmoid(lse - sink_ref[head]) / den)
                ps.append(p.astype(BF16))
            outs.append((q0, vw, jnp.concatenate(ps, axis=0), gates))
        for q0, vw, p_all, gates in outs:
            o_all = _dot(p_all, vw)
            for g in range(group):
                o = o_all[g * blk:(g + 1) * blk] * gates[g]
                o_ref[pl.ds(q0, blk), g * HEAD_DIM:(g + 1) * HEAD_DIM] = o.astype(o_ref.dtype)
        return carry

    lax.fori_loop(0, seq // (blk * SWA_BLOCKS_PER_TRIP), body, 0)


def windowed_gqa(u, slopes, sink):
    b, s, _ = u.shape
    group = N_HEADS // SWA_KV_HEADS
    qw = group * HEAD_DIM
    smem = pl.BlockSpec(memory_space=pltpu.SMEM)
    return pl.pallas_call(
        functools.partial(_swa_kernel, seq=s),
        grid=(b, SWA_KV_HEADS),
        in_specs=[smem, smem,
                  pl.BlockSpec((None, s, qw), lambda n, kv: (n, 0, COL_CQ // group + kv)),
                  pl.BlockSpec((None, s, HEAD_DIM), lambda n, kv: (n, 0, COL_CK + kv)),
                  pl.BlockSpec((None, s, HEAD_DIM), lambda n, kv: (n, 0, COL_CV + kv))],
        out_specs=pl.BlockSpec((None, s, qw), lambda n, kv: (n, 0, kv)),
        out_shape=jax.ShapeDtypeStruct((b, s, GROUP_WIDTH), BF16),
        compiler_params=_params("parallel", "arbitrary"),
        name="windowed_gqa",
    )(slopes, sink, u, u, u)


DIL_TQ = 128
DIL_RES = 4
DIL_NEAR = 160
DIL_NEAR_PAD = 16
DIL_TILES_PER_TRIP = 4


def _dil_bias(d, patterns, slope):
    ad = jnp.abs(d)
    cnt = jnp.zeros(d.shape, F32)
    for window, dil in patterns:
        hit = ((d & (dil - 1)) == 0) & (ad <= window // 2)
        cnt = cnt + jnp.where(hit, 1.0, 0.0)
    bias = jnp.log(jnp.maximum(cnt, 1.0)) - slope * ad.astype(F32)
    return jnp.where(cnt > 0.0, bias, NEG)


def _dil_kernel(slopes_ref, q_ref, k_ref, v_ref, o_ref, tbn_ref, tbf_ref, out_ref, stage_ref, cls_ref, *, seq):
    h = pl.program_id(0)
    sub = seq // DIL_RES

    for xi, x_ref in enumerate((q_ref, k_ref, v_ref)):
        stage_ref[xi] = x_ref[...].astype(F32)
        for r in range(DIL_RES):
            cls_ref[xi, r] = stage_ref[xi, pl.ds(r, sub, stride=DIL_RES), :].astype(BF16)
    qs, ks, vs = ([cls_ref.at[xi, r] for r in range(DIL_RES)] for xi in range(3))
    n_tiles = sub // DIL_TQ
    far_off = sub - DIL_TQ
    near_pat, far_pat = DIL_PATTERNS[:1], DIL_PATTERNS[1:]

    def near_start(a0):
        return min(max(a0 - DIL_NEAR_PAD, 0), sub - DIL_NEAR)

    shifts = sorted({near_start(t * DIL_TQ) - t * DIL_TQ for t in range(n_tiles)})
    assert len(shifts) == tbn_ref.shape[1]

    @pl.when(pl.program_id(1) == 0)
    def _():
        slope = slopes_ref[h]
        qi = lax.broadcasted_iota(jnp.int32, (DIL_TQ, DIL_NEAR), 0)
        kk = lax.broadcasted_iota(jnp.int32, (DIL_TQ, DIL_NEAR), 1)
        for r in range(DIL_RES):
            for si, sh in enumerate(shifts):
                for r2 in range(DIL_RES):
                    d = DIL_RES * (kk + sh - qi) + (r2 - r)
                    tbn_ref[r, si, :, r2 * DIL_NEAR:(r2 + 1) * DIL_NEAR] = _dil_bias(d, near_pat, slope)
        qf = lax.broadcasted_iota(jnp.int32, (DIL_TQ, 128), 0)
        lf = lax.broadcasted_iota(jnp.int32, (DIL_TQ, 128), 1)
        for cb in range((sub + far_off) // 128):
            d = DIL_RES * (lf + (cb * 128 - far_off) - qf)
            tbf_ref[:, cb * 128:(cb + 1) * 128] = _dil_bias(d, far_pat, slope)

    tiles = [(r, t) for r in range(DIL_RES) for t in range(n_tiles)]
    for g0 in range(0, len(tiles), DIL_TILES_PER_TRIP):
        group = tiles[g0:g0 + DIL_TILES_PER_TRIP]
        ss, vcats = [], []
        for r, t in group:
            a0 = t * DIL_TQ
            ws = near_start(a0)
            si = shifts.index(ws - a0)
            kcat = jnp.concatenate([ks[r2][ws:ws + DIL_NEAR, :] for r2 in range(DIL_RES)] + [ks[r][...]], axis=0)
            vcats.append(jnp.concatenate([vs[r2][ws:ws + DIL_NEAR, :] for r2 in range(DIL_RES)] + [vs[r][...]], axis=0))
            bias = jnp.concatenate([tbn_ref[r, si], tbf_ref[:, far_off - a0:far_off - a0 + sub]], axis=1)
            ss.append(_dot_nt(qs[r][a0:a0 + DIL_TQ, :], kcat) * SCALE + bias)
        ps, dens = [], []
        for s in ss:
            m = jnp.max(s, axis=-1, keepdims=True)
            p = jnp.exp(s - m)
            dens.append(jnp.sum(p, axis=-1, keepdims=True))
            ps.append(p.astype(BF16))
        for (r, t), p, vcat, den in zip(group, ps, vcats, dens):
            out_ref[pl.ds(r + DIL_RES * t * DIL_TQ, DIL_TQ, stride=DIL_RES), :] = _dot(p, vcat) / den
    o_ref[...] = out_ref[...].astype(o_ref.dtype)


def dilated_attention(u, slopes):
    b, s, _ = u.shape
    sub = s // DIL_RES
    n_shifts = min(sub // DIL_TQ, 3)
    col = lambda c0: pl.BlockSpec((None, s, HEAD_DIM), lambda h, n: (n, 0, c0 + h))
    return pl.pallas_call(
        functools.partial(_dil_kernel, seq=s),
        grid=(N_HEADS, b),
        in_specs=[pl.BlockSpec(memory_space=pltpu.SMEM),
                  col(COL_DQ), col(COL_DK), col(COL_DV)],
        out_specs=pl.BlockSpec((None, s, HEAD_DIM), lambda h, n: (n, 0, h)),
        out_shape=jax.ShapeDtypeStruct((b, s, GROUP_WIDTH), BF16),
        scratch_shapes=[pltpu.VMEM((DIL_RES, n_shifts, DIL_TQ, DIL_RES * DIL_NEAR), F32),
                        pltpu.VMEM((DIL_TQ, 2 * sub - DIL_TQ), F32),
                        pltpu.VMEM((s, HEAD_DIM), F32),
                        pltpu.VMEM((3, s, HEAD_DIM), F32),
                        pltpu.VMEM((3, DIL_RES, sub, HEAD_DIM), BF16)],
        compiler_params=_params("arbitrary", "arbitrary"),
        name="dilated_attn",
    )(slopes, u, u, u)


def kernel(x, ffn1_norm, ffn1_w_in, ffn1_w_out, mix_norm, w_in, na_rpb, conv_w, conv_b, conv_ln_g,
           conv_ln_b, swa_sink, branch_norm, w_out, ffn2_norm, ffn2_w_in, ffn2_w_out, final_norm):
    b, s, d = x.shape
    depth = w_in.shape[0]
    n_slopes = 2 * N_HEADS
    slopes = 2.0 ** (-8.0 * jnp.arange(1, n_slopes + 1, dtype=F32) / n_slopes)
    sl_c, sl_d = slopes[:N_HEADS], slopes[N_HEADS:]

    fold = lambda w, g: (w * g[:, :, None]).astype(BF16)
    ffn1_wi, ffn1_wo = fold(ffn1_w_in, ffn1_norm), ffn1_w_out.astype(BF16)
    ffn2_wi, ffn2_wo = fold(ffn2_w_in, ffn2_norm), ffn2_w_out.astype(BF16)
    mix_wi = fold(w_in, mix_norm)
    mix_wo = fold(w_out, branch_norm.reshape(depth, -1))

    h = x.reshape(b * s, d)
    hb, ss = prep(h)
    for l in range(depth):
        act = ffn_in(hb, ss, ffn1_wi, l)
        h, hb, ss = res_out(act, ffn1_wo, l, h, 0.5)

        ua, uc = mix_in(hb, ss, mix_wi, l)
        ua = ua.reshape(b, s, ATTN_COLS)
        uc = uc.reshape(b, s, 2 * GROUP_WIDTH)
        oa = neighbourhood_attention(ua, na_bias(na_rpb[l]))
        ob = conformer_conv(uc, conv_w[l], conv_b[l], conv_ln_g[l], conv_ln_b[l])
        oc = windowed_gqa(ua, sl_c, swa_sink[l])
        od = dilated_attention(ua, sl_d)
        outs = [o.reshape(b * s, GROUP_WIDTH) for o in (oa, ob, oc, od)]
        h, hb, ss = mix_out(outs, mix_wo, l, h)

        act = ffn_in(hb, ss, ffn2_wi, l)
        h, hb, ss = res_out(act, ffn2_wo, l, h, 0.5)
    return final_rms(h, final_norm).reshape(b, s, d)
```

```python
import functools

import jax
import jax.numpy as jnp
from jax import lax
from jax.experimental import pallas as pl
from jax.experimental.pallas import tpu as pltpu

D_MODEL = 4096
HEAD_DIM = 128
GROUP_WIDTH = 1024
N_HEADS = GROUP_WIDTH // HEAD_DIM
GRID_W = 64
NA_ROWS = 8
NA_COLS = 16
NA_ROWS_PER_TRIP = 16
CONV_WIDTH = 31
SUBLANES = 8
CONV_HALO = 16
SWA_KV_HEADS = 2
SWA_RADIUS = 128
SWA_BLOCKS_PER_TRIP = 2
DIL_PATTERNS = ((128, 1), (512, 4), (2048, 16))
EPS = 1e-6
NEG = -1e30
SCALE = HEAD_DIM ** -0.5

F32 = jnp.float32
BF16 = jnp.bfloat16

W_CONV_LO, W_CONV_HI = 24, 40
COL_AQ, COL_AK, COL_AV = 0, 8, 16
COL_CQ, COL_CK, COL_CV = 24, 32, 34
COL_DQ, COL_DK, COL_DV = 36, 44, 52
ATTN_COLS = 60 * HEAD_DIM

VMEM_LIMIT = 56 * 1024 * 1024


def _params(*sem):
    return pltpu.CompilerParams(dimension_semantics=sem, vmem_limit_bytes=VMEM_LIMIT)


def _dot(a, b):
    return jnp.dot(a, b, preferred_element_type=F32)


def _dot_nt(a, b):
    return lax.dot_general(a, b, (((1,), (1,)), ((), ())), preferred_element_type=F32)


def _rms(x, g):
    ms = jnp.mean(x * x, axis=-1, keepdims=True)
    return x * lax.rsqrt(ms + EPS) * g


STAT_LANES = 128


def _row_rsqrt(ss_ref, d):
    return lax.rsqrt(ss_ref[:, 0:1] / d + EPS)


def _emit_stream(y, j, o_ref, ob_ref, ss_ref):
    o_ref[...] = y
    ob_ref[...] = y.astype(BF16)
    part = jnp.broadcast_to(jnp.sum(y * y, axis=-1, keepdims=True), ss_ref.shape)

    @pl.when(j == 0)
    def _():
        ss_ref[...] = part

    @pl.when(j > 0)
    def _():
        ss_ref[...] += part


def _stream_out(t, n, tm, tn):
    specs = [pl.BlockSpec((tm, tn), lambda i, j: (i, j)),
             pl.BlockSpec((tm, tn), lambda i, j: (i, j)),
             pl.BlockSpec((tm, STAT_LANES), lambda i, j: (i, 0))]
    shapes = [jax.ShapeDtypeStruct((t, n), F32),
              jax.ShapeDtypeStruct((t, n), BF16),
              jax.ShapeDtypeStruct((t, STAT_LANES), F32)]
    return specs, shapes


def _prep_kernel(x_ref, ob_ref, ss_ref):
    x = x_ref[...]
    ob_ref[...] = x.astype(BF16)
    ss_ref[...] = jnp.broadcast_to(jnp.sum(x * x, axis=-1, keepdims=True), ss_ref.shape)


def prep(x, tm=512):
    t, d = x.shape
    return pl.pallas_call(
        _prep_kernel,
        grid=(t // tm,),
        in_specs=[pl.BlockSpec((tm, d), lambda i: (i, 0))],
        out_specs=[pl.BlockSpec((tm, d), lambda i: (i, 0)),
                   pl.BlockSpec((tm, STAT_LANES), lambda i: (i, 0))],
        out_shape=[jax.ShapeDtypeStruct((t, d), BF16),
                   jax.ShapeDtypeStruct((t, STAT_LANES), F32)],
        compiler_params=_params("parallel"),
        name="prep",
    )(x)


def _mix_in_kernel(hb_ref, ss_ref, w_ref, o_ref):
    y = _dot(hb_ref[...], w_ref[...]) * _row_rsqrt(ss_ref, hb_ref.shape[-1])
    o_ref[...] = y.astype(o_ref.dtype)


def _mix_in_heads_kernel(hb_ref, ss_ref, w_ref, o_ref):
    y = _dot(hb_ref[...], w_ref[...]) * _row_rsqrt(ss_ref, hb_ref.shape[-1])
    for c in range(o_ref.shape[0]):
        o_ref[c] = y[:, c * HEAD_DIM:(c + 1) * HEAD_DIM].astype(o_ref.dtype)


def _mix_in_heads_call(hb, ss, w, layer, w_tile, n_out, tm, tn):
    t, d = hb.shape
    hpt = tn // HEAD_DIM
    return pl.pallas_call(
        _mix_in_heads_kernel,
        grid=(t // tm, n_out // tn),
        in_specs=[pl.BlockSpec((tm, d), lambda i, j: (i, 0)),
                  pl.BlockSpec((tm, STAT_LANES), lambda i, j: (i, 0)),
                  pl.BlockSpec((None, d, tn), lambda i, j: (layer, 0, w_tile(j)))],
        out_specs=pl.BlockSpec((hpt, tm, HEAD_DIM), lambda i, j: (j, i, 0)),
        out_shape=jax.ShapeDtypeStruct((n_out // HEAD_DIM, t, HEAD_DIM), BF16),
        compiler_params=_params("parallel", "arbitrary"),
        name="mix_in_heads",
    )(hb, ss, w)


def _mix_in_call(hb, ss, w, layer, w_tile, n_out, out_dtype, tm, tn):
    t, d = hb.shape
    return pl.pallas_call(
        _mix_in_kernel,
        grid=(t // tm, n_out // tn),
        in_specs=[pl.BlockSpec((tm, d), lambda i, j: (i, 0)),
                  pl.BlockSpec((tm, STAT_LANES), lambda i, j: (i, 0)),
                  pl.BlockSpec((None, d, tn), lambda i, j: (layer, 0, w_tile(j)))],
        out_specs=pl.BlockSpec((tm, tn), lambda i, j: (i, j)),
        out_shape=jax.ShapeDtypeStruct((t, n_out), out_dtype),
        compiler_params=_params("parallel", "arbitrary"),
        name="mix_in",
    )(hb, ss, w)


def mix_in(hb, ss, w, layer, tm=2048, tn=512):
    n = w.shape[2]
    conv_lo = W_CONV_LO * HEAD_DIM // tn
    n_conv = (W_CONV_HI - W_CONV_LO) * HEAD_DIM // tn
    skip_conv = lambda j: jnp.where(j < conv_lo, j, j + n_conv)
    ua = _mix_in_heads_call(hb, ss, w, layer, skip_conv, n - n_conv * tn, tm, tn)
    uc = _mix_in_call(hb, ss, w, layer, lambda j: j + conv_lo, n_conv * tn, F32, tm, tn)
    return ua, uc


def _ffn_in_kernel(hb_ref, ss_ref, wg_ref, wu_ref, o_ref):
    r = _row_rsqrt(ss_ref, hb_ref.shape[-1])
    x = hb_ref[...]
    gate = _dot(x, wg_ref[...]) * r
    up = _dot(x, wu_ref[...]) * r
    o_ref[...] = (gate * jax.nn.sigmoid(gate) * up).astype(o_ref.dtype)


def ffn_in(hb, ss, w, layer, tm=1024, tf=512):
    t, d = hb.shape
    f = w.shape[2] // 2
    nf = f // tf
    return pl.pallas_call(
        _ffn_in_kernel,
        grid=(t // tm, nf),
        in_specs=[pl.BlockSpec((tm, d), lambda i, j: (i, 0)),
                  pl.BlockSpec((tm, STAT_LANES), lambda i, j: (i, 0)),
                  pl.BlockSpec((None, d, tf), lambda i, j: (layer, 0, j)),
                  pl.BlockSpec((None, d, tf), lambda i, j: (layer, 0, j + nf))],
        out_specs=pl.BlockSpec((tm, tf), lambda i, j: (i, j)),
        out_shape=jax.ShapeDtypeStruct((t, f), BF16),
        compiler_params=_params("parallel", "arbitrary"),
        name="ffn_in",
    )(hb, ss, w, w)


def _res_out_kernel(a_ref, w_ref, h_ref, o_ref, ob_ref, ss_ref, *, scale):
    y = h_ref[...] + scale * _dot(a_ref[...], w_ref[...])
    _emit_stream(y, pl.program_id(1), o_ref, ob_ref, ss_ref)


def res_out(a, w, layer, h, scale, tm=1024, tn=512):
    t, k = a.shape
    n = w.shape[2]
    out_specs, out_shape = _stream_out(t, n, tm, tn)
    return pl.pallas_call(
        functools.partial(_res_out_kernel, scale=scale),
        grid=(t // tm, n // tn),
        in_specs=[pl.BlockSpec((tm, k), lambda i, j: (i, 0)),
                  pl.BlockSpec((None, k, tn), lambda i, j: (layer, 0, j)),
                  pl.BlockSpec((tm, tn), lambda i, j: (i, j))],
        out_specs=out_specs,
        out_shape=out_shape,
        compiler_params=_params("parallel", "arbitrary"),
        name="res_out",
    )(a, w, h)


def _mix_out_kernel(oa_ref, ob_ref, oc_ref, od_ref, w_ref, h_ref, o_ref, hb_ref, ss_ref, rs_ref):
    j = pl.program_id(1)
    groups = (oa_ref, ob_ref, oc_ref, od_ref)

    @pl.when(j == 0)
    def _():
        for gi, r in enumerate(groups):
            x = r[...].astype(F32)
            ms = jnp.mean(x * x, axis=-1, keepdims=True)
            rs_ref[gi] = jnp.broadcast_to(lax.rsqrt(ms + EPS), rs_ref.shape[1:])

    y = h_ref[...]
    for gi, r in enumerate(groups):
        lo = gi * GROUP_WIDTH
        y = y + rs_ref[gi][:, 0:1] * _dot(r[...], w_ref[lo:lo + GROUP_WIDTH, :])
    _emit_stream(y, j, o_ref, hb_ref, ss_ref)


def mix_out(outs, w, layer, h, tm=1024, tn=512):
    t, d = h.shape
    gw = GROUP_WIDTH
    grp = pl.BlockSpec((tm, gw), lambda i, j: (i, 0))
    out_specs, out_shape = _stream_out(t, d, tm, tn)
    return pl.pallas_call(
        _mix_out_kernel,
        grid=(t // tm, d // tn),
        in_specs=[grp, grp, grp, grp,
                  pl.BlockSpec((None, len(outs) * gw, tn), lambda i, j: (layer, 0, j)),
                  pl.BlockSpec((tm, tn), lambda i, j: (i, j))],
        out_specs=out_specs,
        out_shape=out_shape,
        scratch_shapes=[pltpu.VMEM((len(outs), tm, STAT_LANES), F32)],
        compiler_params=_params("parallel", "arbitrary"),
        name="mix_out",
    )(*outs, w, h)


def _final_norm_kernel(h_ref, g_ref, o_ref):
    o_ref[...] = _rms(h_ref[...], g_ref[...])


def final_rms(h, g, tm=512):
    t, d = h.shape
    return pl.pallas_call(
        _final_norm_kernel,
        grid=(t // tm,),
        in_specs=[pl.BlockSpec((tm, d), lambda i: (i, 0)),
                  pl.BlockSpec((1, d), lambda i: (0, 0))],
        out_specs=pl.BlockSpec((tm, d), lambda i: (i, 0)),
        out_shape=jax.ShapeDtypeStruct((t, d), F32),
        compiler_params=_params("parallel"),
        name="final_norm",
    )(h, g.reshape(1, d))


def _na_bias_kernel(rpb_ref, o_ref):
    h = pl.program_id(0)
    qc = lax.broadcasted_iota(jnp.int32, (GRID_W, GRID_W), 0)
    kc = lax.broadcasted_iota(jnp.int32, (GRID_W, GRID_W), 1)
    diff = kc - qc + (NA_COLS - 1)
    cstart = jnp.clip(qc - NA_COLS // 2, 0, GRID_W - NA_COLS)
    col_ok = (kc >= cstart) & (kc < cstart + NA_COLS)
    tiles = []
    for row in range(2 * NA_ROWS - 1):
        tile = jnp.zeros((GRID_W, GRID_W), F32)
        for d in range(2 * NA_COLS - 1):
            tile = jnp.where(diff == d, rpb_ref[h, row, d], tile)
        tiles.append(jnp.where(col_ok, tile, NEG))
    for oi in range(NA_ROWS):
        for kr in range(NA_ROWS):
            o_ref[oi, :, kr * GRID_W:(kr + 1) * GRID_W] = tiles[kr - oi + (NA_ROWS - 1)]


def na_bias(rpb):
    nh = rpb.shape[0]
    return pl.pallas_call(
        _na_bias_kernel,
        grid=(nh,),
        in_specs=[pl.BlockSpec(memory_space=pltpu.SMEM)],
        out_specs=pl.BlockSpec((None, NA_ROWS, GRID_W, NA_ROWS * GRID_W), lambda h: (h, 0, 0, 0)),
        out_shape=jax.ShapeDtypeStruct((nh, NA_ROWS, GRID_W, NA_ROWS * GRID_W), F32),
        compiler_params=_params("arbitrary"),
        name="na_bias",
    )(rpb)


def _na_kernel(q_ref, k_ref, v_ref, bias_ref, o_ref, *, rows):
    win = NA_ROWS * GRID_W

    def body(g, carry):
        ss, vws, q0s = [], [], []
        for t in range(NA_ROWS_PER_TRIP):
            r = g * NA_ROWS_PER_TRIP + t
            start_row = jnp.clip(r - NA_ROWS // 2, 0, rows - NA_ROWS)
            start = pl.multiple_of(start_row * GRID_W, GRID_W)
            q0 = pl.multiple_of(r * GRID_W, GRID_W)
            q = q_ref[pl.ds(q0, GRID_W), :]
            kw = k_ref[pl.ds(start, win), :]
            vws.append(v_ref[pl.ds(start, win), :])
            q0s.append(q0)
            ss.append(_dot_nt(q, kw) * SCALE + bias_ref[r - start_row])
        ps, dens = [], []
        for s in ss:
            m = jnp.max(s, axis=-1, keepdims=True)
            p = jnp.exp(s - m)
            dens.append(jnp.sum(p, axis=-1, keepdims=True))
            ps.append(p.astype(BF16))
        for q0, p, vw, den in zip(q0s, ps, vws, dens):
            o_ref[pl.ds(q0, GRID_W), :] = (_dot(p, vw) / den).astype(o_ref.dtype)
        return carry

    lax.fori_loop(0, rows // NA_ROWS_PER_TRIP, body, 0)


def neighbourhood_attention(u, bias):
    _, b, s, _ = u.shape
    rows = s // GRID_W
    col = lambda c0: pl.BlockSpec((None, None, s, HEAD_DIM), lambda h, n: (c0 + h, n, 0, 0))
    return pl.pallas_call(
        functools.partial(_na_kernel, rows=rows),
        grid=(N_HEADS, b),
        in_specs=[col(COL_AQ), col(COL_AK), col(COL_AV),
                  pl.BlockSpec((None, NA_ROWS, GRID_W, NA_ROWS * GRID_W), lambda h, n: (h, 0, 0, 0))],
        out_specs=pl.BlockSpec((None, s, HEAD_DIM), lambda h, n: (n, 0, h)),
        out_shape=jax.ShapeDtypeStruct((b, s, GROUP_WIDTH), BF16),
        compiler_params=_params("arbitrary", "arbitrary"),
        name="na_attn",
    )(u, u, u, bias)


def _glu(a, g):
    return a * jax.nn.sigmoid(g)


def _conv_kernel(a_ref, g_ref, ap_ref, gp_ref, an_ref, gn_ref, w_ref, b_ref, lg_ref, lb_ref,
                 o_ref, hp_ref, acc_ref, hs_ref, *, ts, n_tiles):
    i = pl.program_id(1)
    c = a_ref.shape[-1]
    prev_ok = (i > 0).astype(F32)
    next_ok = (i < n_tiles - 1).astype(F32)
    hp_ref[0:CONV_HALO, :] = _glu(ap_ref[...], gp_ref[...]) * prev_ok
    hp_ref[CONV_HALO:CONV_HALO + ts, :] = _glu(a_ref[...], g_ref[...])
    hp_ref[CONV_HALO + ts:2 * CONV_HALO + ts, :] = _glu(an_ref[...], gn_ref[...]) * next_ok

    shift = CONV_HALO - CONV_WIDTH // 2
    slab = ts + 2 * CONV_HALO - SUBLANES
    for rem in range(1, SUBLANES):
        hs_ref[rem - 1] = hp_ref[rem:rem + slab, :]
    for cc in range(c // 128):
        lanes = slice(cc * 128, (cc + 1) * 128)
        acc = jnp.zeros((ts, 128), F32)
        for rem in range(SUBLANES):
            for base in range(0, slab - ts + 1, SUBLANES):
                k = base + rem - shift
                if 0 <= k < CONV_WIDTH:
                    if rem == 0:
                        rows = hp_ref[base:base + ts, lanes]
                    else:
                        rows = hs_ref[rem - 1, base:base + ts, lanes]
                    acc = acc + rows * w_ref[k:k + 1, lanes]
        acc_ref[:, lanes] = acc + b_ref[:, lanes]

    hf = acc_ref[...]
    mu = jnp.mean(hf, axis=-1, keepdims=True)
    xc = hf - mu
    var = jnp.mean(xc * xc, axis=-1, keepdims=True)
    y = xc * lax.rsqrt(var + EPS) * lg_ref[...] + lb_ref[...]
    o_ref[...] = (y * jax.nn.sigmoid(y)).astype(o_ref.dtype)


def conformer_conv(uc, w, bias, ln_g, ln_b, ts=256):
    b, s, c2 = uc.shape
    c = c2 // 2
    n_tiles = s // ts
    hb = ts // CONV_HALO
    last = s // CONV_HALO - 1
    cur = lambda half: pl.BlockSpec((None, ts, c), lambda n, i: (n, i, half))
    prev = lambda half: pl.BlockSpec((None, CONV_HALO, c),
                                     lambda n, i: (n, jnp.maximum(i * hb - 1, 0), half))
    nxt = lambda half: pl.BlockSpec((None, CONV_HALO, c),
                                    lambda n, i: (n, jnp.minimum((i + 1) * hb, last), half))
    vec = pl.BlockSpec((1, c), lambda n, i: (0, 0))
    return pl.pallas_call(
        functools.partial(_conv_kernel, ts=ts, n_tiles=n_tiles),
        grid=(b, n_tiles),
        in_specs=[cur(0), cur(1), prev(0), prev(1), nxt(0), nxt(1),
                  pl.BlockSpec((CONV_WIDTH, c), lambda n, i: (0, 0)), vec, vec, vec],
        out_specs=pl.BlockSpec((None, ts, c), lambda n, i: (n, i, 0)),
        out_shape=jax.ShapeDtypeStruct((b, s, c), BF16),
        scratch_shapes=[pltpu.VMEM((ts + 2 * CONV_HALO, c), F32), pltpu.VMEM((ts, c), F32),
                        pltpu.VMEM((SUBLANES - 1, ts + 2 * CONV_HALO - SUBLANES, c), F32)],
        compiler_params=_params("parallel", "arbitrary"),
        name="conformer_conv",
    )(uc, uc, uc, uc, uc, uc, w, bias.reshape(1, c), ln_g.reshape(1, c), ln_b.reshape(1, c))


def _swa_kernel(slopes_ref, sink_ref, q_ref, k_ref, v_ref, o_ref, *, seq):
    kvh = pl.program_id(1)
    blk = SWA_RADIUS
    win = 3 * blk
    group = N_HEADS // SWA_KV_HEADS
    qi = lax.broadcasted_iota(jnp.int32, (blk, win), 0)
    ki = lax.broadcasted_iota(jnp.int32, (blk, win), 1)

    def body(it, carry):
        chains = []
        for t in range(SWA_BLOCKS_PER_TRIP):
            q0 = pl.multiple_of((it * SWA_BLOCKS_PER_TRIP + t) * blk, blk)
            start = pl.multiple_of(jnp.clip(q0 - blk, 0, seq - win), blk)
            kw = k_ref[pl.ds(start, win), :]
            vw = v_ref[pl.ds(start, win), :]
            qs = jnp.concatenate([q_ref[g, pl.ds(q0, blk), :] for g in range(group)], axis=0)
            chains.append((q0, start, vw, _dot_nt(qs, kw) * SCALE))
        outs = []
        for q0, start, vw, s_all in chains:
            dist = jnp.abs(ki - qi + (start - q0))
            valid = dist <= SWA_RADIUS
            distf = dist.astype(F32)
            ps, gates = [], []
            for g in range(group):
                head = kvh * group + g
                s = s_all[g * blk:(g + 1) * blk]
                s = jnp.where(valid, s - slopes_ref[head] * distf, NEG)
                m = jnp.max(s, axis=-1, keepdims=True)
                p = jnp.exp(s - m)
                den = jnp.sum(p, axis=-1, keepdims=True)
                lse = m + jnp.log(den)
                gates.append(jax.nn.sigmoid(lse - sink_ref[head]) / den)
                ps.append(p.astype(BF16))
            outs.append((q0, vw, jnp.concatenate(ps, axis=0), gates))
        for q0, vw, p_all, gates in outs:
            o_all = _dot(p_all, vw)
            for g in range(group):
                o = o_all[g * blk:(g + 1) * blk] * gates[g]
                o_ref[pl.ds(q0, blk), g * HEAD_DIM:(g + 1) * HEAD_DIM] = o.astype(o_ref.dtype)
        return carry

    lax.fori_loop(0, seq // (blk * SWA_BLOCKS_PER_TRIP), body, 0)


def windowed_gqa(u, slopes, sink):
    _, b, s, _ = u.shape
    group = N_HEADS // SWA_KV_HEADS
    qw = group * HEAD_DIM
    smem = pl.BlockSpec(memory_space=pltpu.SMEM)
    return pl.pallas_call(
        functools.partial(_swa_kernel, seq=s),
        grid=(b, SWA_KV_HEADS),
        in_specs=[smem, smem,
                  pl.BlockSpec((group, None, s, HEAD_DIM), lambda n, kv: (COL_CQ // group + kv, n, 0, 0)),
                  pl.BlockSpec((None, None, s, HEAD_DIM), lambda n, kv: (COL_CK + kv, n, 0, 0)),
                  pl.BlockSpec((None, None, s, HEAD_DIM), lambda n, kv: (COL_CV + kv, n, 0, 0))],
        out_specs=pl.BlockSpec((None, s, qw), lambda n, kv: (n, 0, kv)),
        out_shape=jax.ShapeDtypeStruct((b, s, GROUP_WIDTH), BF16),
        compiler_params=_params("parallel", "arbitrary"),
        name="windowed_gqa",
    )(slopes, sink, u, u, u)


DIL_TQ = 128
DIL_RES = 4
DIL_NEAR = 160
DIL_NEAR_PAD = 16
DIL_TILES_PER_TRIP = 4


def _dil_bias(d, patterns, slope):
    ad = jnp.abs(d)
    cnt = jnp.zeros(d.shape, F32)
    for window, dil in patterns:
        hit = ((d & (dil - 1)) == 0) & (ad <= window // 2)
        cnt = cnt + jnp.where(hit, 1.0, 0.0)
    bias = jnp.log(jnp.maximum(cnt, 1.0)) - slope * ad.astype(F32)
    return jnp.where(cnt > 0.0, bias, NEG)


def _dil_kernel(slopes_ref, q_ref, k_ref, v_ref, o_ref, tbn_ref, tbf_ref, out_ref, stage_ref, cls_ref, *, seq):
    h = pl.program_id(0)
    sub = seq // DIL_RES

    for xi, x_ref in enumerate((q_ref, k_ref, v_ref)):
        stage_ref[xi] = x_ref[...].astype(F32)
        for r in range(DIL_RES):
            cls_ref[xi, r] = stage_ref[xi, pl.ds(r, sub, stride=DIL_RES), :].astype(BF16)
    qs, ks, vs = ([cls_ref.at[xi, r] for r in range(DIL_RES)] for xi in range(3))
    n_tiles = sub // DIL_TQ
    far_off = sub - DIL_TQ
    near_pat, far_pat = DIL_PATTERNS[:1], DIL_PATTERNS[1:]

    def near_start(a0):
        return min(max(a0 - DIL_NEAR_PAD, 0), sub - DIL_NEAR)

    shifts = sorted({near_start(t * DIL_TQ) - t * DIL_TQ for t in range(n_tiles)})
    assert len(shifts) == tbn_ref.shape[1]

    @pl.when(pl.program_id(1) == 0)
    def _():
        slope = slopes_ref[h]
        qi = lax.broadcasted_iota(jnp.int32, (DIL_TQ, DIL_NEAR), 0)
        kk = lax.broadcasted_iota(jnp.int32, (DIL_TQ, DIL_NEAR), 1)
        for r in range(DIL_RES):
            for si, sh in enumerate(shifts):
                for r2 in range(DIL_RES):
                    d = DIL_RES * (kk + sh - qi) + (r2 - r)
                    tbn_ref[r, si, :, r2 * DIL_NEAR:(r2 + 1) * DIL_NEAR] = _dil_bias(d, near_pat, slope)
        qf = lax.broadcasted_iota(jnp.int32, (DIL_TQ, 128), 0)
        lf = lax.broadcasted_iota(jnp.int32, (DIL_TQ, 128), 1)
        for cb in range((sub + far_off) // 128):
            d = DIL_RES * (lf + (cb * 128 - far_off) - qf)
            tbf_ref[:, cb * 128:(cb + 1) * 128] = _dil_bias(d, far_pat, slope)

    tiles = [(r, t) for r in range(DIL_RES) for t in range(n_tiles)]
    for g0 in range(0, len(tiles), DIL_TILES_PER_TRIP):
        group = tiles[g0:g0 + DIL_TILES_PER_TRIP]
        ss, vcats = [], []
        for r, t in group:
            a0 = t * DIL_TQ
            ws = near_start(a0)
            si = shifts.index(ws - a0)
            kcat = jnp.concatenate([ks[r2][ws:ws + DIL_NEAR, :] for r2 in range(DIL_RES)] + [ks[r][...]], axis=0)
            vcats.append(jnp.concatenate([vs[r2][ws:ws + DIL_NEAR, :] for r2 in range(DIL_RES)] + [vs[r][...]], axis=0))
            bias = jnp.concatenate([tbn_ref[r, si], tbf_ref[:, far_off - a0:far_off - a0 + sub]], axis=1)
            ss.append(_dot_nt(qs[r][a0:a0 + DIL_TQ, :], kcat) * SCALE + bias)
        ps, dens = [], []
        for s in ss:
            m = jnp.max(s, axis=-1, keepdims=True)
            p = jnp.exp(s - m)
            dens.append(jnp.sum(p, axis=-1, keepdims=True))
            ps.append(p.astype(BF16))
        for (r, t), p, vcat, den in zip(group, ps, vcats, dens):
            out_ref[pl.ds(r + DIL_RES * t * DIL_TQ, DIL_TQ, stride=DIL_RES), :] = _dot(p, vcat) / den
    o_ref[...] = out_ref[...].astype(o_ref.dtype)


def dilated_attention(u, slopes):
    _, b, s, _ = u.shape
    sub = s // DIL_RES
    n_shifts = min(sub // DIL_TQ, 3)
    col = lambda c0: pl.BlockSpec((None, None, s, HEAD_DIM), lambda h, n: (c0 + h, n, 0, 0))
    return pl.pallas_call(
        functools.partial(_dil_kernel, seq=s),
        grid=(N_HEADS, b),
        in_specs=[pl.BlockSpec(memory_space=pltpu.SMEM),
                  col(COL_DQ), col(COL_DK), col(COL_DV)],
        out_specs=pl.BlockSpec((None, s, HEAD_DIM), lambda h, n: (n, 0, h)),
        out_shape=jax.ShapeDtypeStruct((b, s, GROUP_WIDTH), BF16),
        scratch_shapes=[pltpu.VMEM((DIL_RES, n_shifts, DIL_TQ, DIL_RES * DIL_NEAR), F32),
                        pltpu.VMEM((DIL_TQ, 2 * sub - DIL_TQ), F32),
                        pltpu.VMEM((s, HEAD_DIM), F32),
                        pltpu.VMEM((3, s, HEAD_DIM), F32),
                        pltpu.VMEM((3, DIL_RES, sub, HEAD_DIM), BF16)],
        compiler_params=_params("arbitrary", "arbitrary"),
        name="dilated_attn",
    )(slopes, u, u, u)


def kernel(x, ffn1_norm, ffn1_w_in, ffn1_w_out, mix_norm, w_in, na_rpb, conv_w, conv_b, conv_ln_g,
           conv_ln_b, swa_sink, branch_norm, w_out, ffn2_norm, ffn2_w_in, ffn2_w_out, final_norm):
    b, s, d = x.shape
    depth = w_in.shape[0]
    n_slopes = 2 * N_HEADS
    slopes = 2.0 ** (-8.0 * jnp.arange(1, n_slopes + 1, dtype=F32) / n_slopes)
    sl_c, sl_d = slopes[:N_HEADS], slopes[N_HEADS:]

    fold = lambda w, g: (w * g[:, :, None]).astype(BF16)
    ffn1_wi, ffn1_wo = fold(ffn1_w_in, ffn1_norm), ffn1_w_out.astype(BF16)
    ffn2_wi, ffn2_wo = fold(ffn2_w_in, ffn2_norm), ffn2_w_out.astype(BF16)
    mix_wi = fold(w_in, mix_norm)
    mix_wo = fold(w_out, branch_norm.reshape(depth, -1))

    h = x.reshape(b * s, d)
    hb, ss = prep(h)
    for l in range(depth):
        act = ffn_in(hb, ss, ffn1_wi, l)
        h, hb, ss = res_out(act, ffn1_wo, l, h, 0.5)

        ua, uc = mix_in(hb, ss, mix_wi, l)
        ua = ua.reshape(ATTN_COLS // HEAD_DIM, b, s, HEAD_DIM)
        uc = uc.reshape(b, s, 2 * GROUP_WIDTH)
        oa = neighbourhood_attention(ua, na_bias(na_rpb[l]))
        ob = conformer_conv(uc, conv_w[l], conv_b[l], conv_ln_g[l], conv_ln_b[l])
        oc = windowed_gqa(ua, sl_c, swa_sink[l])
        od = dilated_attention(ua, sl_d)
        outs = [o.reshape(b * s, GROUP_WIDTH) for o in (oa, ob, oc, od)]
        h, hb, ss = mix_out(outs, mix_wo, l, h)

        act = ffn_in(hb, ss, ffn2_wi, l)
        h, hb, ss = res_out(act, ffn2_wo, l, h, 0.5)
    return final_rms(h, final_norm).reshape(b, s, d)
```
